```python
import math
import jax
import jax.numpy as jnp
from jax import lax
import numpy as np

D_MODEL = 1024
BATCH = 16
SEQ = 2048
DEPTH = 1
DEC_BATCH = 2
DEC_SEQ = 16384
PAST_LEN = 128

D_MIX = D_MODEL
S5_WIDTH = D_MIX // 2
S5_CH = 16
S5_GROUPS = S5_WIDTH // S5_CH
S5_STATE = 64
ATTN_WIDTH = D_MIX - S5_WIDTH
HEAD_DIM = 64
N_HEADS = ATTN_WIDTH // HEAD_DIM
DILATED_PATTERNS = ((128, 1), (512, 4), (2048, 16))
ATTN_BLOCK = 64
D_IN = S5_WIDTH + 3 * ATTN_WIDTH
D_FF = 2816
CONV_WIDTH = 3
NORM_EPS = 1e-6
NEG_INF = -1e30

kernel_name = "hymba_s5_longnet_encoder"


def rms_norm(x, g):
    x32 = x.astype(jnp.float32)
    return x32 * lax.rsqrt(jnp.mean(x32 * x32, axis=-1, keepdims=True) + NORM_EPS) * g.astype(jnp.float32)


def _complex_scan_combine(e1, e2):
    a1r, a1i, b1r, b1i = e1
    a2r, a2i, b2r, b2i = e2
    ar = a1r * a2r - a1i * a2i
    ai = a1r * a2i + a1i * a2r
    br = a2r * b1r - a2i * b1i + b2r
    bi = a2r * b1i + a2i * b1r + b2i
    return (ar, ai, br, bi)


def s5_mixer(u, a_re, a_im, log_dt, b_re, b_im, c_re, c_im, d_skip, w_glu, b_glu):
    bsz, seq, _ = u.shape
    u32 = u.astype(jnp.float32)
    ug = u32.reshape(bsz, seq, S5_GROUPS, S5_CH)
    y = u32 * d_skip.astype(jnp.float32)
    for direction in range(2):
        ar = a_re[direction].astype(jnp.float32)
        ai = a_im[direction].astype(jnp.float32)
        dt = jnp.exp(log_dt[direction].astype(jnp.float32))[:, None]
        mag = jnp.exp(ar * dt)
        lr = mag * jnp.cos(ai * dt)
        li = mag * jnp.sin(ai * dt)
        den = ar * ar + ai * ai
        nr = lr - 1.0
        fr = (nr * ar + li * ai) / den
        fi = (li * ar - nr * ai) / den
        br = b_re[direction].astype(jnp.float32)
        bi = b_im[direction].astype(jnp.float32)
        bbar_re = fr[:, :, None] * br - fi[:, :, None] * bi
        bbar_im = fr[:, :, None] * bi + fi[:, :, None] * br
        bu_re = jnp.einsum('bsgc,gpc->bsgp', ug, bbar_re)
        bu_im = jnp.einsum('bsgc,gpc->bsgp', ug, bbar_im)
        lam_re = jnp.broadcast_to(lr, bu_re.shape)
        lam_im = jnp.broadcast_to(li, bu_im.shape)
        _, _, h_re, h_im = lax.associative_scan(
            _complex_scan_combine, (lam_re, lam_im, bu_re, bu_im), axis=1, reverse=(direction == 1))
        yd = (jnp.einsum('bsgp,gcp->bsgc', h_re, c_re[direction].astype(jnp.float32))
              - jnp.einsum('bsgp,gcp->bsgc', h_im, c_im[direction].astype(jnp.float32)))
        y = y + yd.reshape(bsz, seq, S5_WIDTH)
    z = jax.nn.gelu(y)
    gate = jax.nn.sigmoid(z @ w_glu.astype(jnp.float32) + b_glu.astype(jnp.float32))
    return z * gate


def dilated_branch(q, k, v, window, dil, slopes):
    bsz, seq, nh, hd = q.shape
    sub_len = seq // dil
    half = window // (2 * dil)
    nb = -(-sub_len // ATTN_BLOCK)
    lp = nb * ATTN_BLOCK

    def to_sub(t):
        return t.reshape(bsz, sub_len, dil, nh, hd).transpose(0, 2, 1, 3, 4)

    qb = jnp.pad(to_sub(q), ((0, 0), (0, 0), (0, lp - sub_len), (0, 0), (0, 0)))
    qb = qb.reshape(bsz, dil, nb, ATTN_BLOCK, nh, hd)

    def key_blocks(t):
        tp = jnp.pad(to_sub(t), ((0, 0), (0, 0), (ATTN_BLOCK, lp - sub_len + ATTN_BLOCK), (0, 0), (0, 0)))
        views = [tp[:, :, o * ATTN_BLOCK:o * ATTN_BLOCK + lp].reshape(bsz, dil, nb, ATTN_BLOCK, nh, hd)
                 for o in range(3)]
        return jnp.concatenate(views, axis=3)

    kb = key_blocks(k)
    vb = key_blocks(v)
    s = jnp.einsum('bgnqhe,bgnkhe->bgnhqk', qb, kb)
    qi = jnp.arange(nb)[:, None] * ATTN_BLOCK + jnp.arange(ATTN_BLOCK)[None, :]
    kj = jnp.arange(nb)[:, None] * ATTN_BLOCK + jnp.arange(3 * ATTN_BLOCK)[None, :] - ATTN_BLOCK
    rel = jnp.abs(qi[:, :, None] - kj[:, None, :])
    valid = (rel <= half) & (kj[:, None, :] >= 0) & (kj[:, None, :] < sub_len)
    bias = -slopes[None, :, None, None] * (dil * rel).astype(jnp.float32)[:, None]
    s = jnp.where(valid[:, None], s + bias, NEG_INF)
    m = jnp.max(s, axis=-1, keepdims=True)
    p = jnp.exp(s - m)
    den = jnp.sum(p, axis=-1, keepdims=True)
    o = jnp.einsum('bgnhqk,bgnkhe->bgnqhe', p / den, vb)
    lse = (m + jnp.log(den))[..., 0].transpose(0, 1, 2, 4, 3)
    o = o.reshape(bsz, dil, lp, nh, hd)[:, :, :sub_len].transpose(0, 2, 1, 3, 4).reshape(bsz, seq, nh, hd)
    lse = lse.reshape(bsz, dil, lp, nh)[:, :, :sub_len].transpose(0, 2, 1, 3).reshape(bsz, seq, nh)
    return o, lse


def dilated_attention(q, k, v):
    slopes = jnp.exp2(-8.0 * jnp.arange(1, N_HEADS + 1, dtype=jnp.float32) / N_HEADS)
    outs, lses = [], []
    for window, dil in DILATED_PATTERNS:
        o, lse = dilated_branch(q, k, v, window, dil, slopes)
        outs.append(o)
        lses.append(lse)
    weights = jax.nn.softmax(jnp.stack(lses, axis=0), axis=0)
    return jnp.sum(weights[..., None] * jnp.stack(outs, axis=0), axis=0)


def mixing_sublayer(x, norm_g, w_in, a_re, a_im, log_dt, b_re, b_im, c_re, c_im, d_skip, w_glu, b_glu,
                    q_norm_g, k_norm_g, ssm_out_g, attn_out_g, w_out):
    bsz, seq, _ = x.shape
    n = rms_norm(x, norm_g).astype(w_in.dtype)
    proj = n @ w_in
    u = proj[..., :S5_WIDTH]
    q, k, v = jnp.split(proj[..., S5_WIDTH:], 3, axis=-1)
    a_out = s5_mixer(u, a_re, a_im, log_dt, b_re, b_im, c_re, c_im, d_skip, w_glu, b_glu)
    q = rms_norm(q.reshape(bsz, seq, N_HEADS, HEAD_DIM), q_norm_g) * (HEAD_DIM ** -0.5)
    k = rms_norm(k.reshape(bsz, seq, N_HEADS, HEAD_DIM), k_norm_g)
    v = v.reshape(bsz, seq, N_HEADS, HEAD_DIM).astype(jnp.float32)
    b_out = dilated_attention(q, k, v).reshape(bsz, seq, ATTN_WIDTH)
    mix = jnp.concatenate([rms_norm(a_out, ssm_out_g), rms_norm(b_out, attn_out_g)], axis=-1)
    return x + (mix.astype(w_out.dtype) @ w_out).astype(x.dtype)


def conv_ffn_sublayer(x, norm_g, w_up, conv_w, conv_b, w_down):
    n = rms_norm(x, norm_g).astype(w_up.dtype)
    h = n @ w_up
    ch = h.shape[-1]
    h = lax.conv_general_dilated(
        h, conv_w.astype(h.dtype)[:, None, :], window_strides=(1,), padding=[(1, 1)],
        dimension_numbers=('NWC', 'WIO', 'NWC'), feature_group_count=ch) + conv_b.astype(h.dtype)
    g, up = jnp.split(h, 2, axis=-1)
    act = jax.nn.silu(g.astype(jnp.float32)) * up.astype(jnp.float32)
    return x + (act.astype(w_down.dtype) @ w_down).astype(x.dtype)


def encoder_layer(x, i, norm_mix_g, w_in, s5_a_re, s5_a_im, s5_log_dt, s5_b_re, s5_b_im, s5_c_re, s5_c_im,
                  s5_d, w_glu, b_glu, q_norm_g, k_norm_g, ssm_out_g, attn_out_g, w_out,
                  norm_ffn_g, w_up, conv_w, conv_b, w_down):
    x = mixing_sublayer(x, norm_mix_g[i], w_in[i], s5_a_re[i], s5_a_im[i], s5_log_dt[i], s5_b_re[i],
                        s5_b_im[i], s5_c_re[i], s5_c_im[i], s5_d[i], w_glu[i], b_glu[i], q_norm_g[i],
                        k_norm_g[i], ssm_out_g[i], attn_out_g[i], w_out[i])
    x = conv_ffn_sublayer(x, norm_ffn_g[i], w_up[i], conv_w[i], conv_b[i], w_down[i])
    return x


def setup_inputs(seed: int = 0) -> dict:
    key = jax.random.key(seed)
    ks = jax.random.split(key, 24)
    f32 = jnp.float32
    nrm = lambda k, shape, scale: jax.random.normal(k, shape, f32) * scale
    n_idx = jnp.arange(S5_STATE, dtype=f32)
    return {
        "x_prompt": jax.random.normal(ks[0], (BATCH, SEQ, D_MODEL), f32),
        "x_sample": jax.random.normal(ks[1], (DEC_BATCH, DEC_SEQ, D_MODEL), f32),
        "norm_mix_g": 1.0 + nrm(ks[2], (DEPTH, D_MODEL), 0.01),
        "w_in": nrm(ks[3], (DEPTH, D_MODEL, D_IN), D_MODEL ** -0.5),
        "s5_a_re": -0.5 + nrm(ks[4], (DEPTH, 2, S5_GROUPS, S5_STATE), 0.01),
        "s5_a_im": math.pi * n_idx + nrm(ks[5], (DEPTH, 2, S5_GROUPS, S5_STATE), 0.01),
        "s5_log_dt": jax.random.uniform(ks[6], (DEPTH, 2, S5_GROUPS), f32, math.log(0.001), math.log(0.1)),
        "s5_b_re": nrm(ks[7], (DEPTH, 2, S5_GROUPS, S5_STATE, S5_CH), (2 * S5_CH) ** -0.5),
        "s5_b_im": nrm(ks[8], (DEPTH, 2, S5_GROUPS, S5_STATE, S5_CH), (2 * S5_CH) ** -0.5),
        "s5_c_re": nrm(ks[9], (DEPTH, 2, S5_GROUPS, S5_CH, S5_STATE), (2 * S5_STATE) ** -0.5),
        "s5_c_im": nrm(ks[10], (DEPTH, 2, S5_GROUPS, S5_CH, S5_STATE), (2 * S5_STATE) ** -0.5),
        "s5_d": nrm(ks[11], (DEPTH, S5_WIDTH), 1.0),
        "w_glu": nrm(ks[12], (DEPTH, S5_WIDTH, S5_WIDTH), S5_WIDTH ** -0.5),
        "b_glu": nrm(ks[13], (DEPTH, S5_WIDTH), 0.01),
        "q_norm_g": 1.0 + nrm(ks[14], (DEPTH, HEAD_DIM), 0.01),
        "k_norm_g": 1.0 + nrm(ks[15], (DEPTH, HEAD_DIM), 0.01),
        "ssm_out_g": 1.0 + nrm(ks[16], (DEPTH, S5_WIDTH), 0.01),
        "attn_out_g": 1.0 + nrm(ks[17], (DEPTH, ATTN_WIDTH), 0.01),
        "w_out": nrm(ks[18], (DEPTH, D_MIX, D_MODEL), D_MIX ** -0.5),
        "norm_ffn_g": 1.0 + nrm(ks[19], (DEPTH, D_MODEL), 0.01),
        "w_up": nrm(ks[20], (DEPTH, D_MODEL, 2 * D_FF), D_MODEL ** -0.5),
        "conv_w": nrm(ks[21], (DEPTH, CONV_WIDTH, 2 * D_FF), CONV_WIDTH ** -0.5),
        "conv_b": nrm(ks[22], (DEPTH, 2 * D_FF), 0.01),
        "w_down": nrm(ks[23], (DEPTH, D_FF, D_MODEL), D_FF ** -0.5),
    }


def reference(x_prompt, x_sample, norm_mix_g, w_in, s5_a_re, s5_a_im, s5_log_dt, s5_b_re, s5_b_im,
              s5_c_re, s5_c_im, s5_d, w_glu, b_glu, q_norm_g, k_norm_g, ssm_out_g, attn_out_g, w_out,
              norm_ffn_g, w_up, conv_w, conv_b, w_down):
    y_prompt = x_prompt
    y_sample = x_sample
    for i in range(DEPTH):
        y_prompt = encoder_layer(y_prompt, i, norm_mix_g, w_in, s5_a_re, s5_a_im, s5_log_dt, s5_b_re, s5_b_im,
                                 s5_c_re, s5_c_im, s5_d, w_glu, b_glu, q_norm_g, k_norm_g, ssm_out_g,
                                 attn_out_g, w_out, norm_ffn_g, w_up, conv_w, conv_b, w_down)
        y_sample = encoder_layer(y_sample, i, norm_mix_g, w_in, s5_a_re, s5_a_im, s5_log_dt, s5_b_re, s5_b_im,
                                 s5_c_re, s5_c_im, s5_d, w_glu, b_glu, q_norm_g, k_norm_g, ssm_out_g,
                                 attn_out_g, w_out, norm_ffn_g, w_up, conv_w, conv_b, w_down)
    return (y_prompt, y_sample)
```

```python
import functools
import math

import jax
import jax.numpy as jnp
from jax import lax
from jax.experimental import pallas as pl
from jax.experimental.pallas import tpu as pltpu

D_MODEL = 1024
S5_WIDTH = 512
S5_CH = 16
S5_GROUPS = 32
S5_STATE = 64
N_STATE = S5_GROUPS * S5_STATE
ATTN_WIDTH = 512
HEAD_DIM = 64
N_HEADS = 8
HEAD_PAIRS = N_HEADS // 2
DILATED_PATTERNS = ((128, 1), (512, 4), (2048, 16))
D_IN = S5_WIDTH + 3 * ATTN_WIDTH
D_FF = 2816
NORM_EPS = 1e-6
NEG_INF = -1e30

SUBLANES = 8
LANES = 128
BF16_ROWS = 16
VMEM_LIMIT = 56 * 1024 * 1024

TM_PROJ = 512
TM_S5 = 256
SCAN_LANES = 256
TQ_ATTN = 1024
Q_BLOCK = 128
HALF_WIN = 64
TM_FFN = 512
TF_FFN = 256

_NT = (((1,), (1,)), ((), ()))


def _cparams(sem):
    return pltpu.CompilerParams(dimension_semantics=sem, vmem_limit_bytes=VMEM_LIMIT)


def _const_spec(shape):
    nd = len(shape)
    return pl.BlockSpec(shape, lambda *_: (0,) * nd)


def _proj_kernel(x_ref, g_ref, w_ref, ones_ref, gq_ref, gk_ref, u_ref, q_ref, k_ref, v_ref):
    x = x_ref[...]
    ms = jnp.mean(x * x, axis=-1, keepdims=True)
    n = (x * lax.rsqrt(ms + NORM_EPS) * g_ref[...]).astype(jnp.bfloat16)
    proj = jnp.dot(n, w_ref[...], preferred_element_type=jnp.float32)
    u_ref[...] = proj[:, :S5_WIDTH]

    def head_norm(t, gain):
        sq = t * t
        hi = sq.astype(jnp.bfloat16)
        lo = (sq - hi.astype(jnp.float32)).astype(jnp.bfloat16)
        tot = (jnp.dot(hi, ones_ref[...], preferred_element_type=jnp.float32)
               + jnp.dot(lo, ones_ref[...], preferred_element_type=jnp.float32))
        return t * lax.rsqrt(tot * (1.0 / HEAD_DIM) + NORM_EPS) * gain

    q = proj[:, S5_WIDTH:S5_WIDTH + ATTN_WIDTH]
    k = proj[:, S5_WIDTH + ATTN_WIDTH:S5_WIDTH + 2 * ATTN_WIDTH]
    q = head_norm(q, gq_ref[...]) * (HEAD_DIM ** -0.5)
    k = head_norm(k, gk_ref[...])
    v = proj[:, S5_WIDTH + 2 * ATTN_WIDTH:]
    for hp in range(HEAD_PAIRS):
        cols = slice(hp * LANES, (hp + 1) * LANES)
        q_ref[hp] = q[:, cols]
        k_ref[hp] = k[:, cols]
        v_ref[hp] = v[:, cols]


def _proj(x2, g, w_in, ones_blk, gq, gk):
    t = x2.shape[0]
    tm = TM_PROJ
    row = lambda w: pl.BlockSpec((tm, w), lambda i: (i, 0))
    slab = pl.BlockSpec((HEAD_PAIRS, tm, LANES), lambda i: (0, i, 0))
    out = jax.ShapeDtypeStruct((HEAD_PAIRS, t, LANES), jnp.float32)
    return pl.pallas_call(
        _proj_kernel,
        grid=(t // tm,),
        in_specs=[row(D_MODEL), _const_spec((1, D_MODEL)), _const_spec((D_MODEL, D_IN)),
                  _const_spec((ATTN_WIDTH, ATTN_WIDTH)), _const_spec((1, ATTN_WIDTH)),
                  _const_spec((1, ATTN_WIDTH))],
        out_specs=[row(S5_WIDTH), slab, slab, slab],
        out_shape=[jax.ShapeDtypeStruct((t, S5_WIDTH), jnp.float32), out, out, out],
        compiler_params=_cparams(("parallel",)),
        name="proj",
    )(x2, g, w_in, ones_blk, gq, gk)


def _s5_scan(bu_scr, tab_ref, carry_scr, reverse, tm):
    groups = tm // SUBLANES
    for c in range(N_STATE // SCAN_LANES):
        cre = pl.ds(c * SCAN_LANES, SCAN_LANES)
        cim = pl.ds(N_STATE + c * SCAN_LANES, SCAN_LANES)
        pwr, pwi = tab_ref[0, :, cre], tab_ref[1, :, cre]
        steps = [(k, tab_ref[2 + 2 * j, :, cre], tab_ref[3 + 2 * j, :, cre])
                 for j, k in enumerate((1, 2, 4))]

        def body(i, carry):
            cr, ci = carry
            g = (groups - 1 - i) if reverse else i
            rows = pl.ds(pl.multiple_of(g * SUBLANES, SUBLANES), SUBLANES)
            xr = bu_scr[rows, cre]
            xi = bu_scr[rows, cim]
            for k, lr, li in steps:
                shift = (SUBLANES - k) if reverse else k
                sr = pltpu.roll(xr, shift, 0)
                si = pltpu.roll(xi, shift, 0)
                xr, xi = xr + (lr * sr - li * si), xi + (lr * si + li * sr)
            xr, xi = xr + (pwr * cr - pwi * ci), xi + (pwr * ci + pwi * cr)
            bu_scr[rows, cre] = xr
            bu_scr[rows, cim] = xi
            edge = 0 if reverse else SUBLANES - 1
            shape = (SUBLANES, SCAN_LANES)
            return (jnp.broadcast_to(xr[edge:edge + 1, :], shape),
                    jnp.broadcast_to(xi[edge:edge + 1, :], shape))

        cr, ci = lax.fori_loop(0, groups, body, (carry_scr[:, cre], carry_scr[:, cim]))
        carry_scr[:, cre] = cr
        carry_scr[:, cim] = ci


def _s5_state_pass(u_ref, bblk_ref, cblk_ref, tab_ref, bu_scr, carry_scr, reverse, tm):
    @pl.when(pl.program_id(1) == 0)
    def _():
        carry_scr[...] = jnp.zeros_like(carry_scr)

    u = u_ref[0]
    bu_scr[...] = jnp.dot(u.astype(jnp.bfloat16), bblk_ref[...], preferred_element_type=jnp.float32)
    _s5_scan(bu_scr, tab_ref, carry_scr, reverse, tm)
    return u, jnp.dot(bu_scr[...].astype(jnp.bfloat16), cblk_ref[...], preferred_element_type=jnp.float32)


def _s5_fwd_kernel(u_ref, bblk_ref, cblk_ref, tab_ref, y_ref, bu_scr, carry_scr, *, tm):
    _, y = _s5_state_pass(u_ref, bblk_ref, cblk_ref, tab_ref, bu_scr, carry_scr, False, tm)
    y_ref[0] = y


def _s5_bwd_kernel(u_ref, yf_ref, bblk_ref, cblk_ref, tab_ref, d_ref, wglu_ref, bglu_ref, g_ref,
                   a_ref, bu_scr, carry_scr, *, tm):
    u, yb = _s5_state_pass(u_ref, bblk_ref, cblk_ref, tab_ref, bu_scr, carry_scr, True, tm)
    y = u * d_ref[...] + yf_ref[0] + yb
    z = jax.nn.gelu(y)
    gate = jax.nn.sigmoid(
        jnp.dot(z.astype(jnp.bfloat16), wglu_ref[...], preferred_element_type=jnp.float32) + bglu_ref[...])
    a = z * gate
    ms = jnp.mean(a * a, axis=-1, keepdims=True)
    a_ref[0] = (a * lax.rsqrt(ms + NORM_EPS) * g_ref[...]).astype(a_ref.dtype)


def _s5(u, tabs, d_skip, w_glu, b_glu, g_out):
    b, s, _ = u.shape
    tm = TM_S5
    nt = s // tm
    fwd_map = lambda ib, it: (ib, it, 0)
    bwd_map = lambda ib, it: (ib, nt - 1 - it, 0)
    scratch = [pltpu.VMEM((tm, 2 * N_STATE), jnp.float32), pltpu.VMEM((SUBLANES, 2 * N_STATE), jnp.float32)]
    wspecs = [_const_spec((S5_WIDTH, 2 * N_STATE)), _const_spec((2 * N_STATE, S5_WIDTH)),
              _const_spec((8, SUBLANES, N_STATE))]
    (bf, cf, tf), (bb, cb, tb) = tabs
    yf = pl.pallas_call(
        functools.partial(_s5_fwd_kernel, tm=tm),
        grid=(b, nt),
        in_specs=[pl.BlockSpec((1, tm, S5_WIDTH), fwd_map)] + wspecs,
        out_specs=pl.BlockSpec((1, tm, S5_WIDTH), fwd_map),
        out_shape=jax.ShapeDtypeStruct((b, s, S5_WIDTH), jnp.float32),
        scratch_shapes=scratch,
        compiler_params=_cparams(("parallel", "arbitrary")),
        name="s5_fwd",
    )(u, bf, cf, tf)
    return pl.pallas_call(
        functools.partial(_s5_bwd_kernel, tm=tm),
        grid=(b, nt),
        in_specs=[pl.BlockSpec((1, tm, S5_WIDTH), bwd_map), pl.BlockSpec((1, tm, S5_WIDTH), bwd_map)]
        + wspecs + [_const_spec((1, S5_WIDTH)), _const_spec((S5_WIDTH, S5_WIDTH)),
                    _const_spec((1, S5_WIDTH)), _const_spec((1, S5_WIDTH))],
        out_specs=pl.BlockSpec((1, tm, S5_WIDTH), bwd_map),
        out_shape=jax.ShapeDtypeStruct((b, s, S5_WIDTH), jnp.bfloat16),
        scratch_shapes=scratch,
        compiler_params=_cparams(("parallel", "arbitrary")),
        name="s5_bwd",
    )(u, yf, bb, cb, tb, d_skip, w_glu, b_glu, g_out)


def _s5_tables(a_re, a_im, log_dt, b_re, b_im, c_re, c_im, direction):
    f32 = jnp.float32
    ar = a_re[direction].astype(f32)
    ai = a_im[direction].astype(f32)
    dt = jnp.exp(log_dt[direction].astype(f32))[:, None]
    mag = jnp.exp(ar * dt)
    lr = mag * jnp.cos(ai * dt)
    li = mag * jnp.sin(ai * dt)
    den = ar * ar + ai * ai
    nr = lr - 1.0
    fr = (nr * ar + li * ai) / den
    fi = (li * ar - nr * ai) / den
    br = b_re[direction].astype(f32)
    bi = b_im[direction].astype(f32)
    bbar_re = fr[:, :, None] * br - fi[:, :, None] * bi
    bbar_im = fr[:, :, None] * bi + fi[:, :, None] * br
    eye = jnp.eye(S5_GROUPS, dtype=f32)
    to_b = lambda m: jnp.einsum('gpc,gh->gchp', m, eye).reshape(S5_WIDTH, N_STATE)
    to_c = lambda m: jnp.einsum('gcp,gh->gphc', m, eye).reshape(N_STATE, S5_WIDTH)
    bblk = jnp.concatenate([to_b(bbar_re), to_b(bbar_im)], axis=1).astype(jnp.bfloat16)
    cblk = jnp.concatenate([to_c(c_re[direction].astype(f32)), -to_c(c_im[direction].astype(f32))],
                           axis=0).astype(jnp.bfloat16)

    lr = lr.reshape(1, N_STATE)
    li = li.reshape(1, N_STATE)
    cmul = lambda a, b: (a[0] * b[0] - a[1] * b[1], a[0] * b[1] + a[1] * b[0])
    powers = [(lr, li)]
    for _ in range(SUBLANES - 1):
        powers.append(cmul(powers[-1], (lr, li)))
    order = powers[::-1] if direction == 1 else powers
    rows = [jnp.concatenate([p[0] for p in order], axis=0), jnp.concatenate([p[1] for p in order], axis=0)]
    ridx = jnp.arange(SUBLANES)[:, None]
    for k in (1, 2, 4):
        keep = (ridx < SUBLANES - k) if direction == 1 else (ridx >= k)
        rows.append(jnp.where(keep, powers[k - 1][0], 0.0))
        rows.append(jnp.where(keep, powers[k - 1][1], 0.0))
    return bblk, cblk, jnp.stack(rows, axis=0)


def _attn_kernel(q_ref, kp_ref, km_ref, kn_ref, vp_ref, vm_ref, vn_ref, g_ref, o_ref,
                 acc_scr, m_scr, l_scr, *, seq_len, tq):
    t0 = pl.program_id(1) * tq
    acc_scr[...] = jnp.zeros_like(acc_scr)
    l_scr[...] = jnp.zeros_like(l_scr)
    m_scr[...] = jnp.full_like(m_scr, NEG_INF)
    lane = lax.broadcasted_iota(jnp.int32, (1, LANES), 1)
    first_head = lane < HEAD_DIM

    def rows(start, size, dil):
        return pl.ds(start, size) if dil == 1 else pl.ds(start, size, stride=dil)

    def block(dil, r, i0, qb, first, last):
        width = qb + 2 * HALF_WIN
        own_rows = rows(r + dil * i0, qb, dil)

        def window(prev_ref, main_ref, next_ref, hp):
            if first:
                lo = prev_ref[hp, 0, rows(tq - HALF_WIN * dil + r, HALF_WIN, dil), :]
            else:
                lo = main_ref[hp, 0, rows(r + dil * (i0 - HALF_WIN), HALF_WIN, dil), :]
            mid = main_ref[hp, 0, own_rows, :]
            if last:
                hi = next_ref[hp, 0, rows(r, HALF_WIN, dil), :]
            else:
                hi = main_ref[hp, 0, rows(r + dil * (i0 + qb), HALF_WIN, dil), :]
            return jnp.concatenate([lo, mid, hi], axis=0).astype(jnp.bfloat16)

        qi = lax.broadcasted_iota(jnp.int32, (qb, width), 0)
        kj = lax.broadcasted_iota(jnp.int32, (qb, width), 1) - HALF_WIN
        rel = jnp.abs(qi - kj)
        kpos = kj + (t0 // dil + i0)
        valid = (rel <= HALF_WIN) & (kpos >= 0) & (kpos < seq_len // dil)
        dist = (dil * rel).astype(jnp.float32)
        for hp in range(HEAD_PAIRS):
            q2 = q_ref[hp, 0, own_rows, :]
            k2 = window(kp_ref, km_ref, kn_ref, hp)
            v2 = window(vp_ref, vm_ref, vn_ref, hp)
            ms, ls, os_ = [], [], []
            for sub in range(2):
                h = 2 * hp + sub
                slope = 2.0 ** (-8.0 * (h + 1) / N_HEADS)
                own = first_head if sub == 0 else jnp.logical_not(first_head)
                qh = jnp.where(own, q2, 0.0).astype(jnp.bfloat16)
                s = lax.dot_general(qh, k2, _NT, preferred_element_type=jnp.float32)
                s = jnp.where(valid, s - slope * dist, NEG_INF)
                m = jnp.max(s, axis=-1, keepdims=True)
                p = jnp.exp(s - m)
                ls.append(jnp.sum(p, axis=-1, keepdims=True))
                ms.append(m)
                os_.append(jnp.dot(p.astype(jnp.bfloat16), v2, preferred_element_type=jnp.float32))
            m_blk = jnp.where(first_head, ms[0], ms[1])
            l_blk = jnp.where(first_head, ls[0], ls[1])
            o_blk = jnp.where(first_head, os_[0], os_[1])
            m_old = m_scr[hp, own_rows, :]
            m_new = jnp.maximum(m_old, m_blk)
            alpha = jnp.exp(m_old - m_new)
            beta = jnp.exp(m_blk - m_new)
            l_scr[hp, own_rows, :] = alpha * l_scr[hp, own_rows, :] + beta * l_blk
            acc_scr[hp, own_rows, :] = alpha * acc_scr[hp, own_rows, :] + beta * o_blk
            m_scr[hp, own_rows, :] = m_new

    for _, dil in DILATED_PATTERNS:
        n_sub = tq // dil
        qb = min(Q_BLOCK, n_sub)
        nblk = n_sub // qb

        def per_residue(r, _, dil=dil, qb=qb, nblk=nblk):
            if nblk == 1:
                block(dil, r, 0, qb, True, True)
                return 0
            block(dil, r, 0, qb, True, False)
            if nblk > 2:
                def mid(j, _):
                    block(dil, r, pl.multiple_of(j * qb, qb), qb, False, False)
                    return 0
                lax.fori_loop(1, nblk - 1, mid, 0)
            block(dil, r, (nblk - 1) * qb, qb, False, True)
            return 0

        if dil == 1:
            per_residue(0, 0)
        else:
            lax.fori_loop(0, dil, per_residue, 0)

    o = jnp.concatenate([acc_scr[hp] / l_scr[hp] for hp in range(HEAD_PAIRS)], axis=1)
    ms = jnp.mean(o * o, axis=-1, keepdims=True)
    o_ref[0] = (o * lax.rsqrt(ms + NORM_EPS) * g_ref[...]).astype(o_ref.dtype)


def _attn(q, k, v, g_out):
    _, b, s, _ = q.shape
    tq = TQ_ATTN
    nt = s // tq
    blk = (HEAD_PAIRS, 1, tq, LANES)
    main = pl.BlockSpec(blk, lambda ib, it: (0, ib, it, 0))
    prev = pl.BlockSpec(blk, lambda ib, it: (0, ib, jnp.maximum(it - 1, 0), 0))
    nxt = pl.BlockSpec(blk, lambda ib, it: (0, ib, jnp.minimum(it + 1, nt - 1), 0))
    return pl.pallas_call(
        functools.partial(_attn_kernel, seq_len=s, tq=tq),
        grid=(b, nt),
        in_specs=[main, prev, main, nxt, prev, main, nxt, _const_spec((1, ATTN_WIDTH))],
        out_specs=pl.BlockSpec((1, tq, ATTN_WIDTH), lambda ib, it: (ib, it, 0)),
        out_shape=jax.ShapeDtypeStruct((b, s, ATTN_WIDTH), jnp.bfloat16),
        scratch_shapes=[pltpu.VMEM((HEAD_PAIRS, tq, LANES), jnp.float32)] * 3,
        compiler_params=_cparams(("parallel", "parallel")),
        name="attn",
    )(q, k, k, k, v, v, v, g_out)


def _outproj_kernel(x_ref, a_ref, b_ref, wa_ref, wb_ref, g_ref, x1_ref, n_ref):
    x1 = (x_ref[...]
          + jnp.dot(a_ref[...], wa_ref[...], preferred_element_type=jnp.float32)
          + jnp.dot(b_ref[...], wb_ref[...], preferred_element_type=jnp.float32))
    x1_ref[...] = x1
    ms = jnp.mean(x1 * x1, axis=-1, keepdims=True)
    n_ref[...] = (x1 * lax.rsqrt(ms + NORM_EPS) * g_ref[...]).astype(n_ref.dtype)


def _outproj(x2, a_n, b_n, wa, wb, g):
    t = x2.shape[0]
    tm = TM_PROJ
    row = lambda w: pl.BlockSpec((tm, w), lambda i: (i, 0))
    return pl.pallas_call(
        _outproj_kernel,
        grid=(t // tm,),
        in_specs=[row(D_MODEL), row(S5_WIDTH), row(ATTN_WIDTH), _const_spec((S5_WIDTH, D_MODEL)),
                  _const_spec((ATTN_WIDTH, D_MODEL)), _const_spec((1, D_MODEL))],
        out_specs=[row(D_MODEL), row(D_MODEL)],
        out_shape=[jax.ShapeDtypeStruct((t, D_MODEL), jnp.float32),
                   jax.ShapeDtypeStruct((t, D_MODEL), jnp.bfloat16)],
        compiler_params=_cparams(("parallel",)),
        name="outproj",
    )(x2, a_n, b_n, wa, wb, g)


def _ffn_kernel(nm_ref, np_ref, nn_ref, x1_ref, wup_ref, cw_ref, cb_ref, wdn_ref, o_ref, *,
                tiles_per_seq, tm):
    pos = pl.program_id(0) % tiles_per_seq
    halo = BF16_ROWS
    prev = jnp.where(pos == 0, jnp.zeros_like(np_ref[...]), np_ref[...])
    nxt = jnp.where(pos == tiles_per_seq - 1, jnp.zeros_like(nn_ref[...]), nn_ref[...])
    nh = jnp.concatenate([prev, nm_ref[...], nxt], axis=0)
    rows = tm + 2 * halo

    def conv(h, off):
        cols = pl.ds(off, TF_FFN)
        c = (pltpu.roll(h, 1, 0) * cw_ref[0:1, cols] + h * cw_ref[1:2, cols]
             + pltpu.roll(h, rows - 1, 0) * cw_ref[2:3, cols] + cb_ref[:, cols])
        return c[halo:halo + tm]

    acc = x1_ref[...]
    for j in range(D_FF // TF_FFN):
        off = j * TF_FFN
        hg = jnp.dot(nh, wup_ref[:, pl.ds(off, TF_FFN)], preferred_element_type=jnp.float32)
        hu = jnp.dot(nh, wup_ref[:, pl.ds(D_FF + off, TF_FFN)], preferred_element_type=jnp.float32)
        g = conv(hg, off)
        up = conv(hu, D_FF + off)
        act = (g * jax.nn.sigmoid(g) * up).astype(jnp.bfloat16)
        acc = acc + jnp.dot(act, wdn_ref[pl.ds(off, TF_FFN), :], preferred_element_type=jnp.float32)
    o_ref[...] = acc


def _ffn(n2, x1, w_up, conv_w, conv_b, w_down, seq_len):
    t = n2.shape[0]
    tm = TM_FFN
    halo = BF16_ROWS
    hb = tm // halo
    nhb = t // halo
    row = lambda w: pl.BlockSpec((tm, w), lambda i: (i, 0))
    prev = pl.BlockSpec((halo, D_MODEL), lambda i: (jnp.maximum(i * hb - 1, 0), 0))
    nxt = pl.BlockSpec((halo, D_MODEL), lambda i: (jnp.minimum((i + 1) * hb, nhb - 1), 0))
    return pl.pallas_call(
        functools.partial(_ffn_kernel, tiles_per_seq=seq_len // tm, tm=tm),
        grid=(t // tm,),
        in_specs=[row(D_MODEL), prev, nxt, row(D_MODEL), _const_spec((D_MODEL, 2 * D_FF)),
                  _const_spec((3, 2 * D_FF)), _const_spec((1, 2 * D_FF)), _const_spec((D_FF, D_MODEL))],
        out_specs=row(D_MODEL),
        out_shape=jax.ShapeDtypeStruct((t, D_MODEL), jnp.float32),
        compiler_params=_cparams(("parallel",)),
        name="ffn",
    )(n2, n2, n2, x1, w_up, conv_w, conv_b, w_down)


def _layer(x, p):
    b, s, d = x.shape
    x2 = x.reshape(b * s, d)
    u, q, k, v = _proj(x2, p["norm_mix_g"], p["w_in"], p["ones_blk"], p["gq"], p["gk"])
    a_n = _s5(u.reshape(b, s, S5_WIDTH), p["s5_tabs"], p["s5_d"], p["w_glu"], p["b_glu"], p["ssm_out_g"])
    shp = (HEAD_PAIRS, b, s, LANES)
    b_n = _attn(q.reshape(shp), k.reshape(shp), v.reshape(shp), p["attn_out_g"])
    x1, n2 = _outproj(x2, a_n.reshape(b * s, S5_WIDTH), b_n.reshape(b * s, ATTN_WIDTH),
                      p["w_out_a"], p["w_out_b"], p["norm_ffn_g"])
    y = _ffn(n2, x1, p["w_up"], p["conv_w"], p["conv_b"], p["w_down"], s)
    return y.reshape(b, s, d)


def kernel(x_prompt, x_sample, norm_mix_g, w_in, s5_a_re, s5_a_im, s5_log_dt, s5_b_re, s5_b_im, s5_c_re, s5_c_im, s5_d, w_glu, b_glu, q_norm_g, k_norm_g, ssm_out_g, attn_out_g, w_out, norm_ffn_g, w_up, conv_w, conv_b, w_down):
    depth = w_in.shape[0]
    f32, bf16 = jnp.float32, jnp.bfloat16
    head_id = jnp.arange(ATTN_WIDTH) // HEAD_DIM
    ones_blk = (head_id[:, None] == head_id[None, :]).astype(bf16)
    y_prompt, y_sample = x_prompt, x_sample
    for i in range(depth):
        p = {
            "norm_mix_g": norm_mix_g[i].astype(f32)[None],
            "w_in": w_in[i].astype(bf16),
            "ones_blk": ones_blk,
            "gq": jnp.tile(q_norm_g[i].astype(f32), N_HEADS)[None],
            "gk": jnp.tile(k_norm_g[i].astype(f32), N_HEADS)[None],
            "s5_tabs": [_s5_tables(s5_a_re[i], s5_a_im[i], s5_log_dt[i], s5_b_re[i], s5_b_im[i],
                                   s5_c_re[i], s5_c_im[i], direction) for direction in range(2)],
            "s5_d": s5_d[i].astype(f32)[None],
            "w_glu": w_glu[i].astype(bf16),
            "b_glu": b_glu[i].astype(f32)[None],
            "ssm_out_g": ssm_out_g[i].astype(f32)[None],
            "attn_out_g": attn_out_g[i].astype(f32)[None],
            "w_out_a": w_out[i, :S5_WIDTH].astype(bf16),
            "w_out_b": w_out[i, S5_WIDTH:].astype(bf16),
            "norm_ffn_g": norm_ffn_g[i].astype(f32)[None],
            "w_up": w_up[i].astype(bf16),
            "conv_w": conv_w[i].astype(f32),
            "conv_b": conv_b[i].astype(f32)[None],
            "w_down": w_down[i].astype(bf16),
        }
        y_prompt = _layer(y_prompt, p)
        y_sample = _layer(y_sample, p)
    return (y_prompt, y_sample)
```

```python
import functools

import jax
import jax.numpy as jnp
from jax import lax
from jax.experimental import pallas as pl
from jax.experimental.pallas import tpu as pltpu

D_MODEL = 1024
S5_WIDTH = 512
S5_CH = 16
S5_GROUPS = 32
S5_STATE = 64
ATTN_WIDTH = 512
HEAD_DIM = 64
N_HEADS = 8
HEAD_PAIRS = N_HEADS // 2
DILATED_PATTERNS = ((128, 1), (512, 4), (2048, 16))
D_IN = S5_WIDTH + 3 * ATTN_WIDTH
D_FF = 2816
NORM_EPS = 1e-6
NEG_INF = -1e30

LANES = 128
BF16_ROWS = 16
VMEM_LIMIT = 56 * 1024 * 1024

S5_SLABS = S5_WIDTH // LANES
SLAB_GROUPS = LANES // S5_CH
S5_CHUNK = 16
S5_BLOCK = S5_CHUNK * S5_CH
MAX_SCAN_STEPS = 10

TM_PROJ = 512
TQ_ATTN = 1024
Q_BLOCK = 128
HALF_WIN = 64
TM_FFN = 512
TF_FFN = 256

_NT = (((1,), (1,)), ((), ()))


def _cparams(sem):
    return pltpu.CompilerParams(dimension_semantics=sem, vmem_limit_bytes=VMEM_LIMIT)


def _const_spec(shape):
    nd = len(shape)
    return pl.BlockSpec(shape, lambda *_: (0,) * nd)


def _proj_kernel(x_ref, g_ref, w_ref, ones_ref, gq_ref, gk_ref, u_ref, q_ref, k_ref, v_ref):
    x = x_ref[...]
    ms = jnp.mean(x * x, axis=-1, keepdims=True)
    n = (x * lax.rsqrt(ms + NORM_EPS) * g_ref[...]).astype(jnp.bfloat16)
    proj = jnp.dot(n, w_ref[...], preferred_element_type=jnp.float32)

    def head_norm(t, gain):
        sq = t * t
        hi = sq.astype(jnp.bfloat16)
        lo = (sq - hi.astype(jnp.float32)).astype(jnp.bfloat16)
        tot = (jnp.dot(hi, ones_ref[...], preferred_element_type=jnp.float32)
               + jnp.dot(lo, ones_ref[...], preferred_element_type=jnp.float32))
        return t * lax.rsqrt(tot * (1.0 / HEAD_DIM) + NORM_EPS) * gain

    u = proj[:, :S5_WIDTH]
    q = proj[:, S5_WIDTH:S5_WIDTH + ATTN_WIDTH]
    k = proj[:, S5_WIDTH + ATTN_WIDTH:S5_WIDTH + 2 * ATTN_WIDTH]
    q = head_norm(q, gq_ref[...]) * (HEAD_DIM ** -0.5)
    k = head_norm(k, gk_ref[...])
    v = proj[:, S5_WIDTH + 2 * ATTN_WIDTH:]
    for s in range(S5_SLABS):
        cols = slice(s * LANES, (s + 1) * LANES)
        u_ref[s] = u[:, cols]
        q_ref[s] = q[:, cols]
        k_ref[s] = k[:, cols]
        v_ref[s] = v[:, cols]


def _proj(x2, g, w_in, ones_blk, gq, gk):
    t = x2.shape[0]
    tm = TM_PROJ
    slab = pl.BlockSpec((HEAD_PAIRS, tm, LANES), lambda i: (0, i, 0))
    out = jax.ShapeDtypeStruct((HEAD_PAIRS, t, LANES), jnp.float32)
    return pl.pallas_call(
        _proj_kernel,
        grid=(t // tm,),
        in_specs=[pl.BlockSpec((tm, D_MODEL), lambda i: (i, 0)), _const_spec((1, D_MODEL)),
                  _const_spec((D_MODEL, D_IN)), _const_spec((ATTN_WIDTH, ATTN_WIDTH)),
                  _const_spec((1, ATTN_WIDTH)), _const_spec((1, ATTN_WIDTH))],
        out_specs=[slab, slab, slab, slab],
        out_shape=[out, out, out, out],
        compiler_params=_cparams(("parallel",)),
        name="proj",
    )(x2, g, w_in, ones_blk, gq, gk)


def _chunk_scan(xr, xi, tab_ref, gi, col0, n_chunks, reverse):
    lane = lax.broadcasted_iota(jnp.int32, (1, n_chunks), 1)
    k, step = 1, 0
    while k < n_chunks:
        mr = tab_ref[gi, :, col0 + 2 * step:col0 + 2 * step + 1]
        mi = tab_ref[gi, :, col0 + 2 * step + 1:col0 + 2 * step + 2]
        keep = (lane < n_chunks - k) if reverse else (lane >= k)
        shift = (n_chunks - k) if reverse else k
        sr = jnp.where(keep, pltpu.roll(xr, shift, 1), 0.0)
        si = jnp.where(keep, pltpu.roll(xi, shift, 1), 0.0)
        xr, xi = xr + (mr * sr - mi * si), xi + (mr * si + mi * sr)
        k, step = 2 * k, step + 1
    return xr, xi


def _s5_kernel(u_ref, toep_ref, p_ref, q_ref, tab_ref, y_ref, xt_scr, yt_scr, *, n_chunks):
    n = n_chunks
    for t in range(S5_CHUNK):
        rows = pl.ds(t, n, stride=S5_CHUNK)
        xt_scr[t] = u_ref[0, 0, rows, :].T.astype(jnp.bfloat16)
    lane = lax.broadcasted_iota(jnp.int32, (1, n), 1)
    ns = S5_STATE
    def per_group(gi, _):
        ch = pl.ds(pl.multiple_of(gi * S5_CH, S5_CH), S5_CH)
        x = jnp.concatenate([xt_scr[t, ch, :] for t in range(S5_CHUNK)], axis=0)
        y = jnp.dot(toep_ref[gi], x, preferred_element_type=jnp.float32)
        st = jnp.dot(p_ref[gi], x, preferred_element_type=jnp.float32)
        fr, fi = _chunk_scan(st[0:ns], st[ns:2 * ns], tab_ref, gi, 0, n, False)
        br, bi = _chunk_scan(st[2 * ns:3 * ns], st[3 * ns:4 * ns], tab_ref, gi, 2 * MAX_SCAN_STEPS, n, True)
        prev = lambda a: jnp.where(lane >= 1, pltpu.roll(a, 1, 1), 0.0)
        nxt = lambda a: jnp.where(lane < n - 1, pltpu.roll(a, n - 1, 1), 0.0)
        h = jnp.concatenate([prev(fr), prev(fi), nxt(br), nxt(bi)], axis=0).astype(jnp.bfloat16)
        y = y + jnp.dot(q_ref[gi], h, preferred_element_type=jnp.float32)
        for t in range(S5_CHUNK):
            yt_scr[t, ch, :] = y[t * S5_CH:(t + 1) * S5_CH, :]
        return 0

    lax.fori_loop(0, SLAB_GROUPS, per_group, 0)
    for t in range(S5_CHUNK):
        y_ref[0, 0, pl.ds(t, n, stride=S5_CHUNK), :] = yt_scr[t].T


def _s5(u, mats):
    _, b, s, _ = u.shape
    n = s // S5_CHUNK
    toep, pmat, qmat, tab = mats
    io = pl.BlockSpec((1, 1, s, LANES), lambda isl, ib: (isl, ib, 0, 0))
    wspec = pl.BlockSpec((SLAB_GROUPS, S5_BLOCK, S5_BLOCK), lambda isl, ib: (isl, 0, 0))
    return pl.pallas_call(
        functools.partial(_s5_kernel, n_chunks=n),
        grid=(S5_SLABS, b),
        in_specs=[io, wspec, wspec, wspec,
                  pl.BlockSpec((SLAB_GROUPS, S5_STATE, LANES), lambda isl, ib: (isl, 0, 0))],
        out_specs=io,
        out_shape=jax.ShapeDtypeStruct(u.shape, jnp.float32),
        scratch_shapes=[pltpu.VMEM((S5_CHUNK, LANES, n), jnp.bfloat16),
                        pltpu.VMEM((S5_CHUNK, LANES, n), jnp.float32)],
        compiler_params=_cparams(("parallel", "parallel")),
        name="s5",
    )(u, toep, pmat, qmat, tab)


def _s5_matrices(a_re, a_im, log_dt, b_re, b_im, c_re, c_im, d_skip):
    f32 = jnp.float32
    hp = lax.Precision.HIGHEST
    cmul = lambda a, b: (a[0] * b[0] - a[1] * b[1], a[0] * b[1] + a[1] * b[0])
    lags, p_rows, q_cols, tab_cols = [], [], [], []
    for direction in range(2):
        ar = a_re[direction].astype(f32)
        ai = a_im[direction].astype(f32)
        dt = jnp.exp(log_dt[direction].astype(f32))[:, None]
        mag = jnp.exp(ar * dt)
        lam = (mag * jnp.cos(ai * dt), mag * jnp.sin(ai * dt))
        den = ar * ar + ai * ai
        nr = lam[0] - 1.0
        fr = (nr * ar + lam[1] * ai) / den
        fi = (lam[1] * ar - nr * ai) / den
        br = b_re[direction].astype(f32)
        bi = b_im[direction].astype(f32)
        bbar = (fr[:, :, None] * br - fi[:, :, None] * bi, fr[:, :, None] * bi + fi[:, :, None] * br)
        cr = c_re[direction].astype(f32)
        ci = c_im[direction].astype(f32)
        powers = [(jnp.ones_like(lam[0]), jnp.zeros_like(lam[0]))]
        for _ in range(S5_CHUNK):
            powers.append(cmul(powers[-1], lam))
        pw = (jnp.stack([p[0] for p in powers]), jnp.stack([p[1] for p in powers]))
        cl = (cr[None] * pw[0][:, :, None, :] - ci[None] * pw[1][:, :, None, :],
              cr[None] * pw[1][:, :, None, :] + ci[None] * pw[0][:, :, None, :])
        kern = (jnp.einsum('kgap,gpc->kgac', cl[0], bbar[0], precision=hp)
                - jnp.einsum('kgap,gpc->kgac', cl[1], bbar[1], precision=hp))
        lags.append(kern[:S5_CHUNK])
        order = pw[0][:S5_CHUNK], pw[1][:S5_CHUNK]
        if direction == 0:
            order = order[0][::-1], order[1][::-1]
        lb = (order[0][:, :, :, None] * bbar[0][None] - order[1][:, :, :, None] * bbar[1][None],
              order[0][:, :, :, None] * bbar[1][None] + order[1][:, :, :, None] * bbar[0][None])
        to_p = lambda m: m.transpose(1, 2, 0, 3).reshape(S5_GROUPS, S5_STATE, S5_BLOCK)
        p_rows += [to_p(lb[0]), to_p(lb[1])]
        sel = slice(1, S5_CHUNK + 1)
        qz = (cl[0][sel], cl[1][sel]) if direction == 0 else (cl[0][sel][::-1], cl[1][sel][::-1])
        to_q = lambda m: m.transpose(1, 0, 2, 3).reshape(S5_GROUPS, S5_BLOCK, S5_STATE)
        q_cols += [to_q(qz[0]), -to_q(qz[1])]
        step = powers[S5_CHUNK]
        for _ in range(MAX_SCAN_STEPS):
            tab_cols += [step[0], step[1]]
            step = cmul(step, step)
    eye = jnp.eye(S5_CH, dtype=f32)
    centre = lags[0][0] + lags[1][0] + d_skip.astype(f32).reshape(S5_GROUPS, S5_CH)[:, :, None] * eye
    fwd_lags = lags[0][1:]
    bwd_lags = lags[1][1:][::-1]
    by_lag = jnp.concatenate([bwd_lags, centre[None], fwd_lags], axis=0)
    tt = jnp.arange(S5_CHUNK)
    idx = tt[:, None] - tt[None, :] + (S5_CHUNK - 1)
    toep = by_lag[idx].transpose(2, 0, 3, 1, 4).reshape(S5_GROUPS, S5_BLOCK, S5_BLOCK)
    bf16 = jnp.bfloat16
    pmat = jnp.concatenate(p_rows, axis=1)
    qmat = jnp.concatenate(q_cols, axis=2)
    pad = jnp.zeros((S5_GROUPS, S5_STATE, LANES - len(tab_cols)), f32)
    tab = jnp.concatenate([jnp.stack(tab_cols, axis=-1), pad], axis=-1)
    return toep.astype(bf16), pmat.astype(bf16), qmat.astype(bf16), tab


def _attn_kernel(q_ref, kp_ref, km_ref, kn_ref, vp_ref, vm_ref, vn_ref, g_ref, o_ref,
                 acc_scr, m_scr, l_scr, *, seq_len, tq):
    t0 = pl.program_id(1) * tq
    acc_scr[...] = jnp.zeros_like(acc_scr)
    l_scr[...] = jnp.zeros_like(l_scr)
    m_scr[...] = jnp.full_like(m_scr, NEG_INF)
    lane = lax.broadcasted_iota(jnp.int32, (1, LANES), 1)
    first_head = lane < HEAD_DIM

    def rows(start, size, dil):
        return pl.ds(start, size) if dil == 1 else pl.ds(start, size, stride=dil)

    def block(dil, r, i0, qb, first, last):
        width = qb + 2 * HALF_WIN
        own_rows = rows(r + dil * i0, qb, dil)

        def window(prev_ref, main_ref, next_ref, hp):
            if first:
                lo = prev_ref[hp, 0, rows(tq - HALF_WIN * dil + r, HALF_WIN, dil), :]
            else:
                lo = main_ref[hp, 0, rows(r + dil * (i0 - HALF_WIN), HALF_WIN, dil), :]
            mid = main_ref[hp, 0, own_rows, :]
            if last:
                hi = next_ref[hp, 0, rows(r, HALF_WIN, dil), :]
            else:
                hi = main_ref[hp, 0, rows(r + dil * (i0 + qb), HALF_WIN, dil), :]
            return jnp.concatenate([lo, mid, hi], axis=0).astype(jnp.bfloat16)

        qi = lax.broadcasted_iota(jnp.int32, (qb, width), 0)
        kj = lax.broadcasted_iota(jnp.int32, (qb, width), 1) - HALF_WIN
        rel = jnp.abs(qi - kj)
        kpos = kj + (t0 // dil + i0)
        valid = (rel <= HALF_WIN) & (kpos >= 0) & (kpos < seq_len // dil)
        dist = (dil * rel).astype(jnp.float32)
        for hp in range(HEAD_PAIRS):
            q2 = q_ref[hp, 0, own_rows, :]
            k2 = window(kp_ref, km_ref, kn_ref, hp)
            v2 = window(vp_ref, vm_ref, vn_ref, hp)
            ms, ls, os_ = [], [], []
            for sub in range(2):
                h = 2 * hp + sub
                slope = 2.0 ** (-8.0 * (h + 1) / N_HEADS)
                own = first_head if sub == 0 else jnp.logical_not(first_head)
                qh = jnp.where(own, q2, 0.0).astype(jnp.bfloat16)
                s = lax.dot_general(qh, k2, _NT, preferred_element_type=jnp.float32)
                s = jnp.where(valid, s - slope * dist, NEG_INF)
                m = jnp.max(s, axis=-1, keepdims=True)
                p = jnp.exp(s - m)
                ls.append(jnp.sum(p, axis=-1, keepdims=True))
                ms.append(m)
                os_.append(jnp.dot(p.astype(jnp.bfloat16), v2, preferred_element_type=jnp.float32))
            m_blk = jnp.where(first_head, ms[0], ms[1])
            l_blk = jnp.where(first_head, ls[0], ls[1])
            o_blk = jnp.where(first_head, os_[0], os_[1])
            m_old = m_scr[hp, own_rows, :]
            m_new = jnp.maximum(m_old, m_blk)
            alpha = jnp.exp(m_old - m_new)
            beta = jnp.exp(m_blk - m_new)
            l_scr[hp, own_rows, :] = alpha * l_scr[hp, own_rows, :] + beta * l_blk
            acc_scr[hp, own_rows, :] = alpha * acc_scr[hp, own_rows, :] + beta * o_blk
            m_scr[hp, own_rows, :] = m_new

    for _, dil in DILATED_PATTERNS:
        n_sub = tq // dil
        qb = min(Q_BLOCK, n_sub)
        nblk = n_sub // qb

        def per_residue(r, _, dil=dil, qb=qb, nblk=nblk):
            if nblk == 1:
                block(dil, r, 0, qb, True, True)
                return 0
            block(dil, r, 0, qb, True, False)
            if nblk > 2:
                def mid(j, _):
                    block(dil, r, pl.multiple_of(j * qb, qb), qb, False, False)
                    return 0
                lax.fori_loop(1, nblk - 1, mid, 0)
            block(dil, r, (nblk - 1) * qb, qb, False, True)
            return 0

        if dil == 1:
            per_residue(0, 0)
        else:
            lax.fori_loop(0, dil, per_residue, 0)

    o = jnp.concatenate([acc_scr[hp] / l_scr[hp] for hp in range(HEAD_PAIRS)], axis=1)
    ms = jnp.mean(o * o, axis=-1, keepdims=True)
    o_ref[0] = (o * lax.rsqrt(ms + NORM_EPS) * g_ref[...]).astype(o_ref.dtype)


def _attn(q, k, v, g_out):
    _, b, s, _ = q.shape
    tq = TQ_ATTN
    nt = s // tq
    blk = (HEAD_PAIRS, 1, tq, LANES)
    main = pl.BlockSpec(blk, lambda ib, it: (0, ib, it, 0))
    prev = pl.BlockSpec(blk, lambda ib, it: (0, ib, jnp.maximum(it - 1, 0), 0))
    nxt = pl.BlockSpec(blk, lambda ib, it: (0, ib, jnp.minimum(it + 1, nt - 1), 0))
    return pl.pallas_call(
        functools.partial(_attn_kernel, seq_len=s, tq=tq),
        grid=(b, nt),
        in_specs=[main, prev, main, nxt, prev, main, nxt, _const_spec((1, ATTN_WIDTH))],
        out_specs=pl.BlockSpec((1, tq, ATTN_WIDTH), lambda ib, it: (ib, it, 0)),
        out_shape=jax.ShapeDtypeStruct((b, s, ATTN_WIDTH), jnp.bfloat16),
        scratch_shapes=[pltpu.VMEM((HEAD_PAIRS, tq, LANES), jnp.float32)] * 3,
        compiler_params=_cparams(("parallel", "parallel")),
        name="attn",
    )(q, k, k, k, v, v, v, g_out)


def _outproj_kernel(x_ref, y_ref, b_ref, wglu_ref, bglu_ref, ga_ref, wa_ref, wb_ref, g_ref, x1_ref, n_ref):
    y = jnp.concatenate([y_ref[s] for s in range(S5_SLABS)], axis=1)
    z = jax.nn.gelu(y)
    gate = jax.nn.sigmoid(
        jnp.dot(z.astype(jnp.bfloat16), wglu_ref[...], preferred_element_type=jnp.float32) + bglu_ref[...])
    a = z * gate
    ms = jnp.mean(a * a, axis=-1, keepdims=True)
    a_n = (a * lax.rsqrt(ms + NORM_EPS) * ga_ref[...]).astype(jnp.bfloat16)
    x1 = (x_ref[...]
          + jnp.dot(a_n, wa_ref[...], preferred_element_type=jnp.float32)
          + jnp.dot(b_ref[...], wb_ref[...], preferred_element_type=jnp.float32))
    x1_ref[...] = x1
    ms = jnp.mean(x1 * x1, axis=-1, keepdims=True)
    n_ref[...] = (x1 * lax.rsqrt(ms + NORM_EPS) * g_ref[...]).astype(n_ref.dtype)


def _outproj(x2, y, b_n, w_glu, b_glu, g_a, wa, wb, g):
    t = x2.shape[0]
    tm = TM_PROJ
    row = lambda w: pl.BlockSpec((tm, w), lambda i: (i, 0))
    return pl.pallas_call(
        _outproj_kernel,
        grid=(t // tm,),
        in_specs=[row(D_MODEL), pl.BlockSpec((S5_SLABS, tm, LANES), lambda i: (0, i, 0)), row(ATTN_WIDTH),
                  _const_spec((S5_WIDTH, S5_WIDTH)), _const_spec((1, S5_WIDTH)), _const_spec((1, S5_WIDTH)),
                  _const_spec((S5_WIDTH, D_MODEL)), _const_spec((ATTN_WIDTH, D_MODEL)),
                  _const_spec((1, D_MODEL))],
        out_specs=[row(D_MODEL), row(D_MODEL)],
        out_shape=[jax.ShapeDtypeStruct((t, D_MODEL), jnp.float32),
                   jax.ShapeDtypeStruct((t, D_MODEL), jnp.bfloat16)],
        compiler_params=_cparams(("parallel",)),
        name="outproj",
    )(x2, y, b_n, w_glu, b_glu, g_a, wa, wb, g)


def _ffn_kernel(nm_ref, np_ref, nn_ref, x1_ref, wup_ref, cw_ref, cb_ref, wdn_ref, o_ref, *,
                tiles_per_seq, tm):
    pos = pl.program_id(0) % tiles_per_seq
    halo = BF16_ROWS
    prev = jnp.where(pos == 0, jnp.zeros_like(np_ref[...]), np_ref[...])
    nxt = jnp.where(pos == tiles_per_seq - 1, jnp.zeros_like(nn_ref[...]), nn_ref[...])
    nh = jnp.concatenate([prev, nm_ref[...], nxt], axis=0)
    rows = tm + 2 * halo

    def conv(h, off):
        cols = pl.ds(off, TF_FFN)
        c = (pltpu.roll(h, 1, 0) * cw_ref[0:1, cols] + h * cw_ref[1:2, cols]
             + pltpu.roll(h, rows - 1, 0) * cw_ref[2:3, cols] + cb_ref[:, cols])
        return c[halo:halo + tm]

    acc = x1_ref[...]
    for j in range(D_FF // TF_FFN):
        off = j * TF_FFN
        hg = jnp.dot(nh, wup_ref[:, pl.ds(off, TF_FFN)], preferred_element_type=jnp.float32)
        hu = jnp.dot(nh, wup_ref[:, pl.ds(D_FF + off, TF_FFN)], preferred_element_type=jnp.float32)
        g = conv(hg, off)
        up = conv(hu, D_FF + off)
        act = (g * jax.nn.sigmoid(g) * up).astype(jnp.bfloat16)
        acc = acc + jnp.dot(act, wdn_ref[pl.ds(off, TF_FFN), :], preferred_element_type=jnp.float32)
    o_ref[...] = acc


def _ffn(n2, x1, w_up, conv_w, conv_b, w_down, seq_len):
    t = n2.shape[0]
    tm = TM_FFN
    halo = BF16_ROWS
    hb = tm // halo
    nhb = t // halo
    row = lambda w: pl.BlockSpec((tm, w), lambda i: (i, 0))
    prev = pl.BlockSpec((halo, D_MODEL), lambda i: (jnp.maximum(i * hb - 1, 0), 0))
    nxt = pl.BlockSpec((halo, D_MODEL), lambda i: (jnp.minimum((i + 1) * hb, nhb - 1), 0))
    return pl.pallas_call(
        functools.partial(_ffn_kernel, tiles_per_seq=seq_len // tm, tm=tm),
        grid=(t // tm,),
        in_specs=[row(D_MODEL), prev, nxt, row(D_MODEL), _const_spec((D_MODEL, 2 * D_FF)),
                  _const_spec((3, 2 * D_FF)), _const_spec((1, 2 * D_FF)), _const_spec((D_FF, D_MODEL))],
        out_specs=row(D_MODEL),
        out_shape=jax.ShapeDtypeStruct((t, D_MODEL), jnp.float32),
        compiler_params=_cparams(("parallel",)),
        name="ffn",
    )(n2, n2, n2, x1, w_up, conv_w, conv_b, w_down)


def _layer(x, p):
    b, s, d = x.shape
    x2 = x.reshape(b * s, d)
    u, q, k, v = _proj(x2, p["norm_mix_g"], p["w_in"], p["ones_blk"], p["gq"], p["gk"])
    shp = (HEAD_PAIRS, b, s, LANES)
    y = _s5(u.reshape(shp), p["s5_mats"])
    b_n = _attn(q.reshape(shp), k.reshape(shp), v.reshape(shp), p["attn_out_g"])
    x1, n2 = _outproj(x2, y.reshape(S5_SLABS, b * s, LANES), b_n.reshape(b * s, ATTN_WIDTH),
                      p["w_glu"], p["b_glu"], p["ssm_out_g"], p["w_out_a"], p["w_out_b"], p["norm_ffn_g"])
    out = _ffn(n2, x1, p["w_up"], p["conv_w"], p["conv_b"], p["w_down"], s)
    return out.reshape(b, s, d)


def kernel(x_prompt, x_sample, norm_mix_g, w_in, s5_a_re, s5_a_im, s5_log_dt, s5_b_re, s5_b_im, s5_c_re, s5_c_im, s5_d, w_glu, b_glu, q_norm_g, k_norm_g, ssm_out_g, attn_out_g, w_out, norm_ffn_g, w_up, conv_w, conv_b, w_down):
    depth = w_in.shape[0]
    f32, bf16 = jnp.float32, jnp.bfloat16
    head_id = jnp.arange(ATTN_WIDTH) // HEAD_DIM
    ones_blk = (head_id[:, None] == head_id[None, :]).astype(bf16)
    y_prompt, y_sample = x_prompt, x_sample
    for i in range(depth):
        p = {
            "norm_mix_g": norm_mix_g[i].astype(f32)[None],
            "w_in": w_in[i].astype(bf16),
            "ones_blk": ones_blk,
            "gq": jnp.tile(q_norm_g[i].astype(f32), N_HEADS)[None],
            "gk": jnp.tile(k_norm_g[i].astype(f32), N_HEADS)[None],
            "s5_mats": _s5_matrices(s5_a_re[i], s5_a_im[i], s5_log_dt[i], s5_b_re[i], s5_b_im[i],
                                    s5_c_re[i], s5_c_im[i], s5_d[i]),
            "w_glu": w_glu[i].astype(bf16),
            "b_glu": b_glu[i].astype(f32)[None],
            "ssm_out_g": ssm_out_g[i].astype(f32)[None],
            "attn_out_g": attn_out_g[i].astype(f32)[None],
            "w_out_a": w_out[i, :S5_WIDTH].astype(bf16),
            "w_out_b": w_out[i, S5_WIDTH:].astype(bf16),
            "norm_ffn_g": norm_ffn_g[i].astype(f32)[None],
            "w_up": w_up[i].astype(bf16),
            "conv_w": conv_w[i].astype(f32),
            "conv_b": conv_b[i].astype(f32)[None],
            "w_down": w_down[i].astype(bf16),
        }
        y_prompt = _layer(y_prompt, p)
        y_sample = _layer(y_sample, p)
    return (y_prompt, y_sample)
```

```python
import functools

import jax
import jax.numpy as jnp
from jax import lax
from jax.experimental import pallas as pl
from jax.experimental.pallas import tpu as pltpu

D_MODEL = 1024
S5_WIDTH = 512
S5_CH = 16
S5_GROUPS = 32
S5_STATE = 64
ATTN_WIDTH = 512
HEAD_DIM = 64
N_HEADS = 8
HEAD_PAIRS = N_HEADS // 2
DILATED_PATTERNS = ((128, 1), (512, 4), (2048, 16))
D_IN = S5_WIDTH + 3 * ATTN_WIDTH
D_FF = 2816
NORM_EPS = 1e-6
NEG_INF = -1e30

LANES = 128
BF16_ROWS = 16
VMEM_LIMIT = 56 * 1024 * 1024

S5_SLABS = S5_WIDTH // LANES
SLAB_GROUPS = LANES // S5_CH
S5_CHUNK = 16
S5_BLOCK = S5_CHUNK * S5_CH
MAX_SCAN_STEPS = 10
S5_LANES = 1024

TM_PROJ = 512
TQ_ATTN = 1024
Q_BLOCK = 128
HALF_WIN = 64
TM_FFN = 512
TF_FFN = 256

_NT = (((1,), (1,)), ((), ()))


def _cparams(sem):
    return pltpu.CompilerParams(dimension_semantics=sem, vmem_limit_bytes=VMEM_LIMIT)


def _const_spec(shape):
    nd = len(shape)
    return pl.BlockSpec(shape, lambda *_: (0,) * nd)


def _proj_kernel(x_ref, g_ref, w_ref, ones_ref, gq_ref, gk_ref, u_ref, q_ref, k_ref, v_ref):
    x = x_ref[...]
    ms = jnp.mean(x * x, axis=-1, keepdims=True)
    n = (x * lax.rsqrt(ms + NORM_EPS) * g_ref[...]).astype(jnp.bfloat16)
    proj = jnp.dot(n, w_ref[...], preferred_element_type=jnp.float32)

    def head_norm(t, gain):
        sq = t * t
        hi = sq.astype(jnp.bfloat16)
        lo = (sq - hi.astype(jnp.float32)).astype(jnp.bfloat16)
        tot = (jnp.dot(hi, ones_ref[...], preferred_element_type=jnp.float32)
               + jnp.dot(lo, ones_ref[...], preferred_element_type=jnp.float32))
        return t * lax.rsqrt(tot * (1.0 / HEAD_DIM) + NORM_EPS) * gain

    u = proj[:, :S5_WIDTH]
    q = proj[:, S5_WIDTH:S5_WIDTH + ATTN_WIDTH]
    k = proj[:, S5_WIDTH + ATTN_WIDTH:S5_WIDTH + 2 * ATTN_WIDTH]
    q = head_norm(q, gq_ref[...]) * (HEAD_DIM ** -0.5)
    k = head_norm(k, gk_ref[...])
    v = proj[:, S5_WIDTH + 2 * ATTN_WIDTH:]
    for s in range(S5_SLABS):
        cols = slice(s * LANES, (s + 1) * LANES)
        u_ref[s] = u[:, cols]
        q_ref[s] = q[:, cols]
        k_ref[s] = k[:, cols]
        v_ref[s] = v[:, cols]


def _proj(x2, g, w_in, ones_blk, gq, gk):
    t = x2.shape[0]
    tm = TM_PROJ
    slab = pl.BlockSpec((HEAD_PAIRS, tm, LANES), lambda i: (0, i, 0))
    out = jax.ShapeDtypeStruct((HEAD_PAIRS, t, LANES), jnp.float32)
    return pl.pallas_call(
        _proj_kernel,
        grid=(t // tm,),
        in_specs=[pl.BlockSpec((tm, D_MODEL), lambda i: (i, 0)), _const_spec((1, D_MODEL)),
                  _const_spec((D_MODEL, D_IN)), _const_spec((ATTN_WIDTH, ATTN_WIDTH)),
                  _const_spec((1, ATTN_WIDTH)), _const_spec((1, ATTN_WIDTH))],
        out_specs=[slab, slab, slab, slab],
        out_shape=[out, out, out, out],
        compiler_params=_cparams(("parallel",)),
        name="proj",
    )(x2, g, w_in, ones_blk, gq, gk)


def _chunk_scan(xr, xi, tab_ref, gi, col0, pos, n_chunks, reverse):
    lanes = xr.shape[1]
    k, step = 1, 0
    while k < n_chunks:
        mr = tab_ref[gi, :, col0 + 2 * step:col0 + 2 * step + 1]
        mi = tab_ref[gi, :, col0 + 2 * step + 1:col0 + 2 * step + 2]
        keep = (pos < n_chunks - k) if reverse else (pos >= k)
        shift = (lanes - k) if reverse else k
        sr = jnp.where(keep, pltpu.roll(xr, shift, 1), 0.0)
        si = jnp.where(keep, pltpu.roll(xi, shift, 1), 0.0)
        xr, xi = xr + (mr * sr - mi * si), xi + (mr * si + mi * sr)
        k, step = 2 * k, step + 1
    return xr, xi


def _s5_kernel(u_ref, toep_ref, p_ref, q_ref, tab_ref, y_ref, xt_scr, yt_scr, *, n_chunks, n_seq):
    n = n_chunks
    lanes = n_seq * n
    for b in range(n_seq):
        for t in range(S5_CHUNK):
            rows = pl.ds(t, n, stride=S5_CHUNK)
            xt_scr[t, :, b * n:(b + 1) * n] = u_ref[0, b, rows, :].T.astype(jnp.bfloat16)
    pos = lax.broadcasted_iota(jnp.int32, (1, lanes), 1) % n
    ns = S5_STATE
    def per_group(gi, _):
        ch = pl.ds(pl.multiple_of(gi * S5_CH, S5_CH), S5_CH)
        x = jnp.concatenate([xt_scr[t, ch, :] for t in range(S5_CHUNK)], axis=0)
        y = jnp.dot(toep_ref[gi], x, preferred_element_type=jnp.float32)
        st = jnp.dot(p_ref[gi], x, preferred_element_type=jnp.float32)
        fr, fi = _chunk_scan(st[0:ns], st[ns:2 * ns], tab_ref, gi, 0, pos, n, False)
        br, bi = _chunk_scan(st[2 * ns:3 * ns], st[3 * ns:4 * ns], tab_ref, gi, 2 * MAX_SCAN_STEPS, pos, n, True)
        prev = lambda a: jnp.where(pos >= 1, pltpu.roll(a, 1, 1), 0.0)
        nxt = lambda a: jnp.where(pos < n - 1, pltpu.roll(a, lanes - 1, 1), 0.0)
        h = jnp.concatenate([prev(fr), prev(fi), nxt(br), nxt(bi)], axis=0).astype(jnp.bfloat16)
        y = y + jnp.dot(q_ref[gi], h, preferred_element_type=jnp.float32)
        for t in range(S5_CHUNK):
            yt_scr[t, ch, :] = y[t * S5_CH:(t + 1) * S5_CH, :]
        return 0

    lax.fori_loop(0, SLAB_GROUPS, per_group, 0)
    for b in range(n_seq):
        for t in range(S5_CHUNK):
            y_ref[0, b, pl.ds(t, n, stride=S5_CHUNK), :] = yt_scr[t, :, b * n:(b + 1) * n].T


def _s5(u, mats):
    _, b, s, _ = u.shape
    n = s // S5_CHUNK
    nb = max(1, min(b, S5_LANES // n))
    toep, pmat, qmat, tab = mats
    io = pl.BlockSpec((1, nb, s, LANES), lambda isl, ib: (isl, ib, 0, 0))
    wspec = pl.BlockSpec((SLAB_GROUPS, S5_BLOCK, S5_BLOCK), lambda isl, ib: (isl, 0, 0))
    return pl.pallas_call(
        functools.partial(_s5_kernel, n_chunks=n, n_seq=nb),
        grid=(S5_SLABS, b // nb),
        in_specs=[io, wspec, wspec, wspec,
                  pl.BlockSpec((SLAB_GROUPS, S5_STATE, LANES), lambda isl, ib: (isl, 0, 0))],
        out_specs=io,
        out_shape=jax.ShapeDtypeStruct(u.shape, jnp.float32),
        scratch_shapes=[pltpu.VMEM((S5_CHUNK, LANES, nb * n), jnp.bfloat16),
                        pltpu.VMEM((S5_CHUNK, LANES, nb * n), jnp.float32)],
        compiler_params=_cparams(("parallel", "parallel")),
        name="s5",
    )(u, toep, pmat, qmat, tab)


def _s5_matrices(a_re, a_im, log_dt, b_re, b_im, c_re, c_im, d_skip):
    f32 = jnp.float32
    hp = lax.Precision.HIGHEST
    cmul = lambda a, b: (a[0] * b[0] - a[1] * b[1], a[0] * b[1] + a[1] * b[0])
    lags, p_rows, q_cols, tab_cols = [], [], [], []
    for direction in range(2):
        ar = a_re[direction].astype(f32)
        ai = a_im[direction].astype(f32)
        dt = jnp.exp(log_dt[direction].astype(f32))[:, None]
        mag = jnp.exp(ar * dt)
        lam = (mag * jnp.cos(ai * dt), mag * jnp.sin(ai * dt))
        den = ar * ar + ai * ai
        nr = lam[0] - 1.0
        fr = (nr * ar + lam[1] * ai) / den
        fi = (lam[1] * ar - nr * ai) / den
        br = b_re[direction].astype(f32)
        bi = b_im[direction].astype(f32)
        bbar = (fr[:, :, None] * br - fi[:, :, None] * bi, fr[:, :, None] * bi + fi[:, :, None] * br)
        cr = c_re[direction].astype(f32)
        ci = c_im[direction].astype(f32)
        powers = [(jnp.ones_like(lam[0]), jnp.zeros_like(lam[0]))]
        for _ in range(S5_CHUNK):
            powers.append(cmul(powers[-1], lam))
        pw = (jnp.stack([p[0] for p in powers]), jnp.stack([p[1] for p in powers]))
        cl = (cr[None] * pw[0][:, :, None, :] - ci[None] * pw[1][:, :, None, :],
              cr[None] * pw[1][:, :, None, :] + ci[None] * pw[0][:, :, None, :])
        kern = (jnp.einsum('kgap,gpc->kgac', cl[0], bbar[0], precision=hp)
                - jnp.einsum('kgap,gpc->kgac', cl[1], bbar[1], precision=hp))
        lags.append(kern[:S5_CHUNK])
        order = pw[0][:S5_CHUNK], pw[1][:S5_CHUNK]
        if direction == 0:
            order = order[0][::-1], order[1][::-1]
        lb = (order[0][:, :, :, None] * bbar[0][None] - order[1][:, :, :, None] * bbar[1][None],
              order[0][:, :, :, None] * bbar[1][None] + order[1][:, :, :, None] * bbar[0][None])
        to_p = lambda m: m.transpose(1, 2, 0, 3).reshape(S5_GROUPS, S5_STATE, S5_BLOCK)
        p_rows += [to_p(lb[0]), to_p(lb[1])]
        sel = slice(1, S5_CHUNK + 1)
        qz = (cl[0][sel], cl[1][sel]) if direction == 0 else (cl[0][sel][::-1], cl[1][sel][::-1])
        to_q = lambda m: m.transpose(1, 0, 2, 3).reshape(S5_GROUPS, S5_BLOCK, S5_STATE)
        q_cols += [to_q(qz[0]), -to_q(qz[1])]
        step = powers[S5_CHUNK]
        for _ in range(MAX_SCAN_STEPS):
            tab_cols += [step[0], step[1]]
            step = cmul(step, step)
    eye = jnp.eye(S5_CH, dtype=f32)
    centre = lags[0][0] + lags[1][0] + d_skip.astype(f32).reshape(S5_GROUPS, S5_CH)[:, :, None] * eye
    fwd_lags = lags[0][1:]
    bwd_lags = lags[1][1:][::-1]
    by_lag = jnp.concatenate([bwd_lags, centre[None], fwd_lags], axis=0)
    tt = jnp.arange(S5_CHUNK)
    idx = tt[:, None] - tt[None, :] + (S5_CHUNK - 1)
    toep = by_lag[idx].transpose(2, 0, 3, 1, 4).reshape(S5_GROUPS, S5_BLOCK, S5_BLOCK)
    bf16 = jnp.bfloat16
    pmat = jnp.concatenate(p_rows, axis=1)
    qmat = jnp.concatenate(q_cols, axis=2)
    pad = jnp.zeros((S5_GROUPS, S5_STATE, LANES - len(tab_cols)), f32)
    tab = jnp.concatenate([jnp.stack(tab_cols, axis=-1), pad], axis=-1)
    return toep.astype(bf16), pmat.astype(bf16), qmat.astype(bf16), tab


def _attn_kernel(q_ref, kp_ref, km_ref, kn_ref, vp_ref, vm_ref, vn_ref, g_ref, o_ref,
                 o_scr, lse_scr, bias_scr, qd_scr, kd_scr, vd_scr, *, seq_len, tq):
    t0 = pl.program_id(1) * tq
    lane = lax.broadcasted_iota(jnp.int32, (1, LANES), 1)
    first_head = lane < HEAD_DIM

    for ip, (_, dil) in enumerate(DILATED_PATTERNS):
        qb = min(Q_BLOCK, tq // dil)
        width = qb + 2 * HALF_WIN
        qi = lax.broadcasted_iota(jnp.int32, (qb, width), 0)
        kj = lax.broadcasted_iota(jnp.int32, (qb, width), 1) - HALF_WIN
        rel = jnp.abs(qi - kj)
        dist = (dil * rel).astype(jnp.float32)
        for h in range(N_HEADS):
            slope = 2.0 ** (-8.0 * (h + 1) / N_HEADS)
            bias_scr[ip, h // 2, (h % 2) * qb:(h % 2 + 1) * qb, 0:width] = jnp.where(
                rel <= HALF_WIN, -slope * dist, NEG_INF)

    def block(ip, dil, qb, q_of, k_of, v_of, out_rows, first_key, check_ends):
        width = qb + 2 * HALF_WIN
        if check_ends:
            kpos = lax.broadcasted_iota(jnp.int32, (1, width), 1) + first_key
            in_seq = (kpos >= 0) & (kpos < seq_len // dil)
        for hp in range(HEAD_PAIRS):
            q2 = q_of(hp)
            k2 = k_of(hp)
            v2 = v_of(hp)
            qq = jnp.concatenate([jnp.where(first_head, q2, 0.0), jnp.where(first_head, 0.0, q2)],
                                 axis=0).astype(jnp.bfloat16)
            s = lax.dot_general(qq, k2, _NT, preferred_element_type=jnp.float32)
            s = s + bias_scr[ip, hp, 0:2 * qb, 0:width]
            if check_ends:
                s = jnp.where(in_seq, s, NEG_INF)
            m = jnp.max(s, axis=-1, keepdims=True)
            p = jnp.exp(s - m)
            l = jnp.sum(p, axis=-1, keepdims=True)
            o2 = jnp.dot(p.astype(jnp.bfloat16), v2, preferred_element_type=jnp.float32)
            o2 = o2 * (1.0 / l)
            lse = m + jnp.log(l)
            o_scr[ip, hp, out_rows, :] = jnp.where(first_head, o2[:qb], o2[qb:])
            lse_scr[ip, hp, out_rows, :] = jnp.where(first_head, lse[:qb], lse[qb:])

    def dense_block(i0, first, last):
        qb = Q_BLOCK
        own = pl.ds(i0, qb)

        def window(prev_ref, main_ref, next_ref, hp):
            lo = (prev_ref[hp, 0, pl.ds(tq - HALF_WIN, HALF_WIN), :] if first
                  else main_ref[hp, 0, pl.ds(i0 - HALF_WIN, HALF_WIN), :])
            hi = (next_ref[hp, 0, pl.ds(0, HALF_WIN), :] if last
                  else main_ref[hp, 0, pl.ds(i0 + qb, HALF_WIN), :])
            return jnp.concatenate([lo, main_ref[hp, 0, own, :], hi], axis=0).astype(jnp.bfloat16)

        block(0, 1, qb, lambda hp: q_ref[hp, 0, own, :],
              lambda hp: window(kp_ref, km_ref, kn_ref, hp), lambda hp: window(vp_ref, vm_ref, vn_ref, hp),
              own, t0 + i0 - HALF_WIN, first or last)

    nblk = tq // Q_BLOCK
    dense_block(0, True, False)

    def dense_mid(j, _):
        dense_block(pl.multiple_of(j * Q_BLOCK, Q_BLOCK), False, False)
        return 0

    lax.fori_loop(1, nblk - 1, dense_mid, 0)
    dense_block((nblk - 1) * Q_BLOCK, False, True)

    quarter = tq // 4

    def per_class(r4, _):
        for hp in range(HEAD_PAIRS):
            qd_scr[hp] = q_ref[hp, 0, pl.ds(r4, quarter, stride=4), :]
            for part, (kr, vr) in enumerate(((kp_ref, vp_ref), (km_ref, vm_ref), (kn_ref, vn_ref))):
                dst = pl.ds(part * quarter, quarter)
                kd_scr[hp, dst, :] = kr[hp, 0, pl.ds(r4, quarter, stride=4), :]
                vd_scr[hp, dst, :] = vr[hp, 0, pl.ds(r4, quarter, stride=4), :]

        for i0 in range(0, quarter, Q_BLOCK):
            win = pl.ds(quarter + i0 - HALF_WIN, Q_BLOCK + 2 * HALF_WIN)
            block(1, 4, Q_BLOCK, lambda hp: qd_scr[hp, pl.ds(i0, Q_BLOCK), :],
                  lambda hp: kd_scr[hp, win, :].astype(jnp.bfloat16),
                  lambda hp: vd_scr[hp, win, :].astype(jnp.bfloat16),
                  pl.ds(r4 + 4 * i0, Q_BLOCK, stride=4), t0 // 4 + i0 - HALF_WIN, True)

        n16 = tq // 16

        def per_a(a, _):
            win = pl.ds(a + quarter - 4 * HALF_WIN, n16 + 2 * HALF_WIN, stride=4)
            block(2, 16, n16, lambda hp: qd_scr[hp, pl.ds(a, n16, stride=4), :],
                  lambda hp: kd_scr[hp, win, :].astype(jnp.bfloat16),
                  lambda hp: vd_scr[hp, win, :].astype(jnp.bfloat16),
                  pl.ds(4 * a + r4, n16, stride=16), t0 // 16 - HALF_WIN, True)
            return 0

        lax.fori_loop(0, 4, per_a, 0)
        return 0

    lax.fori_loop(0, 4, per_class, 0)

    outs = []
    for hp in range(HEAD_PAIRS):
        lses = [lse_scr[ip, hp] for ip in range(len(DILATED_PATTERNS))]
        top = functools.reduce(jnp.maximum, lses)
        ws = [jnp.exp(x - top) for x in lses]
        num = sum(w * o_scr[ip, hp] for ip, w in enumerate(ws))
        outs.append(num / sum(ws))
    o = jnp.concatenate(outs, axis=1)
    ms = jnp.mean(o * o, axis=-1, keepdims=True)
    o_ref[0] = (o * lax.rsqrt(ms + NORM_EPS) * g_ref[...]).astype(o_ref.dtype)


def _attn(q, k, v, g_out):
    _, b, s, _ = q.shape
    tq = TQ_ATTN
    nt = s // tq
    npat = len(DILATED_PATTERNS)
    blk = (HEAD_PAIRS, 1, tq, LANES)
    main = pl.BlockSpec(blk, lambda ib, it: (0, ib, it, 0))
    prev = pl.BlockSpec(blk, lambda ib, it: (0, ib, jnp.maximum(it - 1, 0), 0))
    nxt = pl.BlockSpec(blk, lambda ib, it: (0, ib, jnp.minimum(it + 1, nt - 1), 0))
    return pl.pallas_call(
        functools.partial(_attn_kernel, seq_len=s, tq=tq),
        grid=(b, nt),
        in_specs=[main, prev, main, nxt, prev, main, nxt, _const_spec((1, ATTN_WIDTH))],
        out_specs=pl.BlockSpec((1, tq, ATTN_WIDTH), lambda ib, it: (ib, it, 0)),
        out_shape=jax.ShapeDtypeStruct((b, s, ATTN_WIDTH), jnp.bfloat16),
        scratch_shapes=[pltpu.VMEM((npat, HEAD_PAIRS, tq, LANES), jnp.float32),
                        pltpu.VMEM((npat, HEAD_PAIRS, tq, LANES), jnp.float32),
                        pltpu.VMEM((npat, HEAD_PAIRS, 2 * Q_BLOCK, Q_BLOCK + 2 * HALF_WIN), jnp.float32),
                        pltpu.VMEM((HEAD_PAIRS, tq // 4, LANES), jnp.float32),
                        pltpu.VMEM((HEAD_PAIRS, 3 * tq // 4, LANES), jnp.float32),
                        pltpu.VMEM((HEAD_PAIRS, 3 * tq // 4, LANES), jnp.float32)],
        compiler_params=_cparams(("parallel", "parallel")),
        name="attn",
    )(q, k, k, k, v, v, v, g_out)


def _outproj_kernel(x_ref, y_ref, b_ref, wglu_ref, bglu_ref, ga_ref, wa_ref, wb_ref, g_ref, x1_ref, n_ref):
    y = jnp.concatenate([y_ref[s] for s in range(S5_SLABS)], axis=1)
    z = jax.nn.gelu(y)
    gate = jax.nn.sigmoid(
        jnp.dot(z.astype(jnp.bfloat16), wglu_ref[...], preferred_element_type=jnp.float32) + bglu_ref[...])
    a = z * gate
    ms = jnp.mean(a * a, axis=-1, keepdims=True)
    a_n = (a * lax.rsqrt(ms + NORM_EPS) * ga_ref[...]).astype(jnp.bfloat16)
    x1 = (x_ref[...]
          + jnp.dot(a_n, wa_ref[...], preferred_element_type=jnp.float32)
          + jnp.dot(b_ref[...], wb_ref[...], preferred_element_type=jnp.float32))
    x1_ref[...] = x1
    ms = jnp.mean(x1 * x1, axis=-1, keepdims=True)
    n_ref[...] = (x1 * lax.rsqrt(ms + NORM_EPS) * g_ref[...]).astype(n_ref.dtype)


def _outproj(x2, y, b_n, w_glu, b_glu, g_a, wa, wb, g):
    t = x2.shape[0]
    tm = TM_PROJ
    row = lambda w: pl.BlockSpec((tm, w), lambda i: (i, 0))
    return pl.pallas_call(
        _outproj_kernel,
        grid=(t // tm,),
        in_specs=[row(D_MODEL), pl.BlockSpec((S5_SLABS, tm, LANES), lambda i: (0, i, 0)), row(ATTN_WIDTH),
                  _const_spec((S5_WIDTH, S5_WIDTH)), _const_spec((1, S5_WIDTH)), _const_spec((1, S5_WIDTH)),
                  _const_spec((S5_WIDTH, D_MODEL)), _const_spec((ATTN_WIDTH, D_MODEL)),
                  _const_spec((1, D_MODEL))],
        out_specs=[row(D_MODEL), row(D_MODEL)],
        out_shape=[jax.ShapeDtypeStruct((t, D_MODEL), jnp.float32),
                   jax.ShapeDtypeStruct((t, D_MODEL), jnp.bfloat16)],
        compiler_params=_cparams(("parallel",)),
        name="outproj",
    )(x2, y, b_n, w_glu, b_glu, g_a, wa, wb, g)


def _ffn_kernel(nm_ref, np_ref, nn_ref, x1_ref, wup_ref, cw_ref, cb_ref, wdn_ref, o_ref, h_scr, act_scr, *,
                tiles_per_seq, tm):
    pos = pl.program_id(0) % tiles_per_seq
    halo = BF16_ROWS
    prev = jnp.where(pos == 0, jnp.zeros_like(np_ref[...]), np_ref[...])
    nxt = jnp.where(pos == tiles_per_seq - 1, jnp.zeros_like(nn_ref[...]), nn_ref[...])
    nh = jnp.concatenate([prev, nm_ref[...], nxt], axis=0)
    slabs = TF_FFN // LANES

    def conv(h, off, base):
        outs = []
        for s in range(slabs):
            h_scr[base + s] = h[:, s * LANES:(s + 1) * LANES]
            cols = pl.ds(off + s * LANES, LANES)
            outs.append(h_scr[base + s, pl.ds(halo - 1, tm), :] * cw_ref[0:1, cols]
                        + h_scr[base + s, pl.ds(halo, tm), :] * cw_ref[1:2, cols]
                        + h_scr[base + s, pl.ds(halo + 1, tm), :] * cw_ref[2:3, cols]
                        + cb_ref[:, cols])
        return jnp.concatenate(outs, axis=1)

    for j in range(D_FF // TF_FFN):
        off = j * TF_FFN
        base = (j % 2) * 2 * slabs
        hg = jnp.dot(nh, wup_ref[:, pl.ds(off, TF_FFN)], preferred_element_type=jnp.float32)
        hu = jnp.dot(nh, wup_ref[:, pl.ds(D_FF + off, TF_FFN)], preferred_element_type=jnp.float32)
        g = conv(hg, off, base)
        up = conv(hu, D_FF + off, base + slabs)
        act_scr[:, pl.ds(off, TF_FFN)] = (g * jax.nn.sigmoid(g) * up).astype(jnp.bfloat16)
    o_ref[...] = x1_ref[...] + jnp.dot(act_scr[...], wdn_ref[...], preferred_element_type=jnp.float32)


def _ffn(n2, x1, w_up, conv_w, conv_b, w_down, seq_len):
    t = n2.shape[0]
    tm = TM_FFN
    halo = BF16_ROWS
    hb = tm // halo
    nhb = t // halo
    row = lambda w: pl.BlockSpec((tm, w), lambda i: (i, 0))
    prev = pl.BlockSpec((halo, D_MODEL), lambda i: (jnp.maximum(i * hb - 1, 0), 0))
    nxt = pl.BlockSpec((halo, D_MODEL), lambda i: (jnp.minimum((i + 1) * hb, nhb - 1), 0))
    return pl.pallas_call(
        functools.partial(_ffn_kernel, tiles_per_seq=seq_len // tm, tm=tm),
        grid=(t // tm,),
        in_specs=[row(D_MODEL), prev, nxt, row(D_MODEL), _const_spec((D_MODEL, 2 * D_FF)),
                  _const_spec((3, 2 * D_FF)), _const_spec((1, 2 * D_FF)), _const_spec((D_FF, D_MODEL))],
        out_specs=row(D_MODEL),
        out_shape=jax.ShapeDtypeStruct((t, D_MODEL), jnp.float32),
        scratch_shapes=[pltpu.VMEM((4 * TF_FFN // LANES, tm + 2 * halo, LANES), jnp.float32),
                        pltpu.VMEM((tm, D_FF), jnp.bfloat16)],
        compiler_params=_cparams(("parallel",)),
        name="ffn",
    )(n2, n2, n2, x1, w_up, conv_w, conv_b, w_down)


def _layer(x, p):
    b, s, d = x.shape
    x2 = x.reshape(b * s, d)
    u, q, k, v = _proj(x2, p["norm_mix_g"], p["w_in"], p["ones_blk"], p["gq"], p["gk"])
    shp = (HEAD_PAIRS, b, s, LANES)
    y = _s5(u.reshape(shp), p["s5_mats"])
    b_n = _attn(q.reshape(shp), k.reshape(shp), v.reshape(shp), p["attn_out_g"])
    x1, n2 = _outproj(x2, y.reshape(S5_SLABS, b * s, LANES), b_n.reshape(b * s, ATTN_WIDTH),
                      p["w_glu"], p["b_glu"], p["ssm_out_g"], p["w_out_a"], p["w_out_b"], p["norm_ffn_g"])
    out = _ffn(n2, x1, p["w_up"], p["conv_w"], p["conv_b"], p["w_down"], s)
    return out.reshape(b, s, d)


def kernel(x_prompt, x_sample, norm_mix_g, w_in, s5_a_re, s5_a_im, s5_log_dt, s5_b_re, s5_b_im, s5_c_re, s5_c_im, s5_d, w_glu, b_glu, q_norm_g, k_norm_g, ssm_out_g, attn_out_g, w_out, norm_ffn_g, w_up, conv_w, conv_b, w_down):
    depth = w_in.shape[0]
    f32, bf16 = jnp.float32, jnp.bfloat16
    head_id = jnp.arange(ATTN_WIDTH) // HEAD_DIM
    ones_blk = (head_id[:, None] == head_id[None, :]).astype(bf16)
    y_prompt, y_sample = x_prompt, x_sample
    for i in range(depth):
        p = {
            "norm_mix_g": norm_mix_g[i].astype(f32)[None],
            "w_in": w_in[i].astype(bf16),
            "ones_blk": ones_blk,
            "gq": jnp.tile(q_norm_g[i].astype(f32), N_HEADS)[None],
            "gk": jnp.tile(k_norm_g[i].astype(f32), N_HEADS)[None],
            "s5_mats": _s5_matrices(s5_a_re[i], s5_a_im[i], s5_log_dt[i], s5_b_re[i], s5_b_im[i],
                                    s5_c_re[i], s5_c_im[i], s5_d[i]),
            "w_glu": w_glu[i].astype(bf16),
            "b_glu": b_glu[i].astype(f32)[None],
            "ssm_out_g": ssm_out_g[i].astype(f32)[None],
            "attn_out_g": attn_out_g[i].astype(f32)[None],
            "w_out_a": w_out[i, :S5_WIDTH].astype(bf16),
            "w_out_b": w_out[i, S5_WIDTH:].astype(bf16),
            "norm_ffn_g": norm_ffn_g[i].astype(f32)[None],
            "w_up": w_up[i].astype(bf16),
            "conv_w": conv_w[i].astype(f32),
            "conv_b": conv_b[i].astype(f32)[None],
            "w_down": w_down[i].astype(bf16),
        }
        y_prompt = _layer(y_prompt, p)
        y_sample = _layer(y_sample, p)
    return (y_prompt, y_sample)
```

```python
import functools

import jax
import jax.numpy as jnp
from jax import lax
from jax.experimental import pallas as pl
from jax.experimental.pallas import tpu as pltpu

D_MODEL = 1024
S5_WIDTH = 512
S5_CH = 16
S5_GROUPS = 32
S5_STATE = 64
ATTN_WIDTH = 512
HEAD_DIM = 64
N_HEADS = 8
HEAD_PAIRS = N_HEADS // 2
DILATED_PATTERNS = ((128, 1), (512, 4), (2048, 16))
D_IN = S5_WIDTH + 3 * ATTN_WIDTH
D_FF = 2816
NORM_EPS = 1e-6
NEG_INF = -1e30

LANES = 128
BF16_ROWS = 16
VMEM_LIMIT = 56 * 1024 * 1024

S5_SLABS = S5_WIDTH // LANES
SLAB_GROUPS = LANES // S5_CH
S5_CHUNK = 16
S5_BLOCK = S5_CHUNK * S5_CH
MAX_SCAN_STEPS = 10
S5_LANES = 1024

TM_PROJ = 512
TQ_ATTN = 1024
Q_BLOCK = 128
BLOCK_GROUP = 2
HALF_WIN = 64
TM_FFN = 512
TF_FFN = 256

_NT = (((1,), (1,)), ((), ()))


def _cparams(sem):
    return pltpu.CompilerParams(dimension_semantics=sem, vmem_limit_bytes=VMEM_LIMIT)


def _const_spec(shape):
    nd = len(shape)
    return pl.BlockSpec(shape, lambda *_: (0,) * nd)


def _proj_kernel(x_ref, g_ref, w_ref, gq_ref, gk_ref, u_ref, q_ref, k_ref, v_ref):
    x = x_ref[...]
    ms = jnp.mean(x * x, axis=-1, keepdims=True)
    n = (x * lax.rsqrt(ms + NORM_EPS) * g_ref[...]).astype(jnp.bfloat16)
    proj = jnp.dot(n, w_ref[...], preferred_element_type=jnp.float32)

    first_head = lax.broadcasted_iota(jnp.int32, (1, LANES), 1) < HEAD_DIM

    def head_norm(t, gain):
        sq = t * t
        tots = []
        for s in range(HEAD_PAIRS):
            blk = sq[:, s * LANES:(s + 1) * LANES]
            first = jnp.sum(jnp.where(first_head, blk, 0.0), axis=-1, keepdims=True)
            second = jnp.sum(jnp.where(first_head, 0.0, blk), axis=-1, keepdims=True)
            tots.append(jnp.where(first_head, first, second))
        tot = jnp.concatenate(tots, axis=1)
        return t * lax.rsqrt(tot * (1.0 / HEAD_DIM) + NORM_EPS) * gain

    u = proj[:, :S5_WIDTH]
    q = proj[:, S5_WIDTH:S5_WIDTH + ATTN_WIDTH]
    k = proj[:, S5_WIDTH + ATTN_WIDTH:S5_WIDTH + 2 * ATTN_WIDTH]
    q = head_norm(q, gq_ref[...]) * (HEAD_DIM ** -0.5)
    k = head_norm(k, gk_ref[...])
    v = proj[:, S5_WIDTH + 2 * ATTN_WIDTH:]
    for s in range(S5_SLABS):
        cols = slice(s * LANES, (s + 1) * LANES)
        u_ref[s] = u[:, cols]
        q_ref[s] = q[:, cols]
        k_ref[s] = k[:, cols]
        v_ref[s] = v[:, cols]


def _proj(x2, g, w_in, gq, gk):
    t = x2.shape[0]
    tm = TM_PROJ
    slab = pl.BlockSpec((HEAD_PAIRS, tm, LANES), lambda i: (0, i, 0))
    out = jax.ShapeDtypeStruct((HEAD_PAIRS, t, LANES), jnp.float32)
    return pl.pallas_call(
        _proj_kernel,
        grid=(t // tm,),
        in_specs=[pl.BlockSpec((tm, D_MODEL), lambda i: (i, 0)), _const_spec((1, D_MODEL)),
                  _const_spec((D_MODEL, D_IN)), _const_spec((1, ATTN_WIDTH)), _const_spec((1, ATTN_WIDTH))],
        out_specs=[slab, slab, slab, slab],
        out_shape=[out, out, out, out],
        compiler_params=_cparams(("parallel",)),
        name="proj",
    )(x2, g, w_in, gq, gk)


def _chunk_scan(xr, xi, tab_ref, gi, col0, pos, n_chunks, reverse):
    lanes = xr.shape[1]
    k, step = 1, 0
    while k < n_chunks:
        mr = tab_ref[gi, :, col0 + 2 * step:col0 + 2 * step + 1]
        mi = tab_ref[gi, :, col0 + 2 * step + 1:col0 + 2 * step + 2]
        keep = (pos < n_chunks - k) if reverse else (pos >= k)
        shift = (lanes - k) if reverse else k
        sr = jnp.where(keep, pltpu.roll(xr, shift, 1), 0.0)
        si = jnp.where(keep, pltpu.roll(xi, shift, 1), 0.0)
        xr, xi = xr + (mr * sr - mi * si), xi + (mr * si + mi * sr)
        k, step = 2 * k, step + 1
    return xr, xi


def _s5_kernel(u_ref, toep_ref, p_ref, q_ref, tab_ref, y_ref, xt_scr, yt_scr, *, n_chunks, n_seq):
    n = n_chunks
    lanes = n_seq * n
    for b in range(n_seq):
        for t in range(S5_CHUNK):
            rows = pl.ds(t, n, stride=S5_CHUNK)
            xt_scr[t, :, b * n:(b + 1) * n] = u_ref[0, b, rows, :].T.astype(jnp.bfloat16)
    pos = lax.broadcasted_iota(jnp.int32, (1, lanes), 1) % n
    ns = S5_STATE

    def per_group(gi, _):
        ch = pl.ds(pl.multiple_of(gi * S5_CH, S5_CH), S5_CH)
        x = jnp.concatenate([xt_scr[t, ch, :] for t in range(S5_CHUNK)], axis=0)
        y = jnp.dot(toep_ref[gi], x, preferred_element_type=jnp.float32)
        st = jnp.dot(p_ref[gi], x, preferred_element_type=jnp.float32)
        fr, fi = _chunk_scan(st[0:ns], st[ns:2 * ns], tab_ref, gi, 0, pos, n, False)
        br, bi = _chunk_scan(st[2 * ns:3 * ns], st[3 * ns:4 * ns], tab_ref, gi, 2 * MAX_SCAN_STEPS, pos, n, True)
        prev = lambda a: jnp.where(pos >= 1, pltpu.roll(a, 1, 1), 0.0)
        nxt = lambda a: jnp.where(pos < n - 1, pltpu.roll(a, lanes - 1, 1), 0.0)
        h = jnp.concatenate([prev(fr), prev(fi), nxt(br), nxt(bi)], axis=0).astype(jnp.bfloat16)
        y = y + jnp.dot(q_ref[gi], h, preferred_element_type=jnp.float32)
        for t in range(S5_CHUNK):
            yt_scr[t, ch, :] = y[t * S5_CH:(t + 1) * S5_CH, :]
        return 0

    lax.fori_loop(0, SLAB_GROUPS, per_group, 0)
    for b in range(n_seq):
        for t in range(S5_CHUNK):
            y_ref[0, b, pl.ds(t, n, stride=S5_CHUNK), :] = yt_scr[t, :, b * n:(b + 1) * n].T


def _s5(u, mats):
    _, b, s, _ = u.shape
    n = s // S5_CHUNK
    nb = max(1, min(b, S5_LANES // n))
    toep, pmat, qmat, tab = mats
    io = pl.BlockSpec((1, nb, s, LANES), lambda isl, ib: (isl, ib, 0, 0))
    wspec = pl.BlockSpec((SLAB_GROUPS, S5_BLOCK, S5_BLOCK), lambda isl, ib: (isl, 0, 0))
    return pl.pallas_call(
        functools.partial(_s5_kernel, n_chunks=n, n_seq=nb),
        grid=(S5_SLABS, b // nb),
        in_specs=[io, wspec, wspec, wspec,
                  pl.BlockSpec((SLAB_GROUPS, S5_STATE, LANES), lambda isl, ib: (isl, 0, 0))],
        out_specs=io,
        out_shape=jax.ShapeDtypeStruct(u.shape, jnp.float32),
        scratch_shapes=[pltpu.VMEM((S5_CHUNK, LANES, nb * n), jnp.bfloat16),
                        pltpu.VMEM((S5_CHUNK, LANES, nb * n), jnp.float32)],
        compiler_params=_cparams(("parallel", "parallel")),
        name="s5",
    )(u, toep, pmat, qmat, tab)


def _s5_matrices(a_re, a_im, log_dt, b_re, b_im, c_re, c_im, d_skip):
    f32 = jnp.float32
    hp = lax.Precision.HIGHEST
    n, g_, p_ = S5_CHUNK, S5_GROUPS, S5_STATE
    cmul = lambda a, b: (a[0] * b[0] - a[1] * b[1], a[0] * b[1] + a[1] * b[0])
    kern, p_rows, q_cols, tab_cols = [], [], [], []
    for direction in range(2):
        ar = a_re[direction].astype(f32)
        ai = a_im[direction].astype(f32)
        dt = jnp.exp(log_dt[direction].astype(f32))[:, None]
        mag = jnp.exp(ar * dt)
        lam = (mag * jnp.cos(ai * dt), mag * jnp.sin(ai * dt))
        den = ar * ar + ai * ai
        nr = lam[0] - 1.0
        fr = (nr * ar + lam[1] * ai) / den
        fi = (lam[1] * ar - nr * ai) / den
        br = b_re[direction].astype(f32)
        bi = b_im[direction].astype(f32)
        bbar = (fr[:, :, None] * br - fi[:, :, None] * bi, fr[:, :, None] * bi + fi[:, :, None] * br)
        cr = c_re[direction].astype(f32)
        ci = c_im[direction].astype(f32)
        powers = [(jnp.ones_like(lam[0]), jnp.zeros_like(lam[0]))]
        for _ in range(n):
            powers.append(cmul(powers[-1], lam))
        ks = list(range(n - 1, -1, -1)) if direction == 0 else list(range(n))
        pw = (jnp.stack([powers[k][0] for k in ks], axis=-1)[..., None],
              jnp.stack([powers[k][1] for k in ks], axis=-1)[..., None])
        m = cmul(pw, (bbar[0][:, :, None, :], bbar[1][:, :, None, :]))
        m = (m[0].reshape(g_, p_, S5_BLOCK), m[1].reshape(g_, p_, S5_BLOCK))
        p_rows += [m[0], m[1]]
        kern.append(jnp.einsum('gap,gpx->gax', cr, m[0], precision=hp)
                    - jnp.einsum('gap,gpx->gax', ci, m[1], precision=hp))
        ts = [t + 1 for t in range(n)] if direction == 0 else [n - t for t in range(n)]
        lt = (jnp.stack([powers[k][0] for k in ts], axis=1)[:, :, None, :],
              jnp.stack([powers[k][1] for k in ts], axis=1)[:, :, None, :])
        z = cmul((cr[:, None], ci[:, None]), lt)
        q_cols += [z[0].reshape(g_, S5_BLOCK, p_), -z[1].reshape(g_, S5_BLOCK, p_)]
        step = powers[n]
        for _ in range(MAX_SCAN_STEPS):
            tab_cols += [step[0], step[1]]
            step = cmul(step, step)
    last = (n - 1) * S5_CH
    eye = jnp.eye(S5_CH, dtype=f32)
    centre = kern[0][:, :, last:] + kern[1][:, :, :S5_CH] + d_skip.astype(f32).reshape(g_, S5_CH)[:, :, None] * eye
    by_lag = jnp.concatenate([kern[0][:, :, :last], centre, kern[1][:, :, S5_CH:]], axis=-1)
    toep = jnp.stack([by_lag[:, :, (n - 1 - t) * S5_CH:(n - 1 - t) * S5_CH + S5_BLOCK] for t in range(n)],
                     axis=1).reshape(g_, S5_BLOCK, S5_BLOCK)
    bf16 = jnp.bfloat16
    pmat = jnp.concatenate(p_rows, axis=1)
    qmat = jnp.concatenate(q_cols, axis=2)
    pad = jnp.zeros((g_, p_, LANES - len(tab_cols)), f32)
    tab = jnp.concatenate([jnp.stack(tab_cols, axis=-1), pad], axis=-1)
    return toep.astype(bf16), pmat.astype(bf16), qmat.astype(bf16), tab


def _attn_kernel(q_ref, kp_ref, km_ref, kn_ref, vp_ref, vm_ref, vn_ref, g_ref, o_ref,
                 o_scr, lse_scr, bias_scr, qd_scr, kd_scr, vd_scr, s_scr, p_scr, *, seq_len, tq):
    t0 = pl.program_id(1) * tq
    lane = lax.broadcasted_iota(jnp.int32, (1, LANES), 1)
    first_head = lane < HEAD_DIM

    for ip, (_, dil) in enumerate(DILATED_PATTERNS):
        qb = min(Q_BLOCK, tq // dil)
        width = qb + 2 * HALF_WIN
        qi = lax.broadcasted_iota(jnp.int32, (qb, width), 0)
        kj = lax.broadcasted_iota(jnp.int32, (qb, width), 1) - HALF_WIN
        rel = jnp.abs(qi - kj)
        dist = (dil * rel).astype(jnp.float32)
        for h in range(N_HEADS):
            slope = 2.0 ** (-8.0 * (h + 1) / N_HEADS)
            bias_scr[ip, h // 2, (h % 2) * qb:(h % 2 + 1) * qb, 0:width] = jnp.where(
                rel <= HALF_WIN, -slope * dist, NEG_INF)

    def blocks(descs):
        chains = [(bi, hp) for bi in range(len(descs)) for hp in range(HEAD_PAIRS)]
        for c, (bi, hp) in enumerate(chains):
            ip, dil, qb, q_of, k_of, _, _, first_key, check_ends = descs[bi]
            width = qb + 2 * HALF_WIN
            q2 = q_of(hp)
            k2 = k_of(hp)
            qq = jnp.concatenate([jnp.where(first_head, q2, 0.0), jnp.where(first_head, 0.0, q2)],
                                 axis=0).astype(jnp.bfloat16)
            s = lax.dot_general(qq, k2, _NT, preferred_element_type=jnp.float32)
            s = s + bias_scr[ip, hp, 0:2 * qb, 0:width]
            if check_ends:
                kpos = lax.broadcasted_iota(jnp.int32, (1, width), 1) + first_key
                s = jnp.where((kpos >= 0) & (kpos < seq_len // dil), s, NEG_INF)
            s_scr[c, 0:2 * qb, 0:width] = s
        stats = []
        for c, (bi, hp) in enumerate(chains):
            qb = descs[bi][2]
            width = qb + 2 * HALF_WIN
            s = s_scr[c, 0:2 * qb, 0:width]
            m = jnp.max(s, axis=-1, keepdims=True)
            p = jnp.exp(s - m)
            l = jnp.sum(p, axis=-1, keepdims=True)
            p_scr[c, 0:2 * qb, 0:width] = p.astype(jnp.bfloat16)
            stats.append((1.0 / l, m + jnp.log(l)))
        for c, (bi, hp) in enumerate(chains):
            ip, _, qb, _, _, v_of, out_rows, _, _ = descs[bi]
            width = qb + 2 * HALF_WIN
            inv_l, lse = stats[c]
            o2 = jnp.dot(p_scr[c, 0:2 * qb, 0:width], v_of(hp), preferred_element_type=jnp.float32)
            o2 = o2 * inv_l
            o_scr[ip, hp, out_rows, :] = jnp.where(first_head, o2[:qb], o2[qb:])
            lse_scr[ip, hp, out_rows, :] = jnp.where(first_head, lse[:qb], lse[qb:])

    def dense_desc(i0, first, last):
        qb = Q_BLOCK
        own = pl.ds(i0, qb)

        def window(prev_ref, main_ref, next_ref, hp):
            lo = (prev_ref[hp, 0, pl.ds(tq - HALF_WIN, HALF_WIN), :] if first
                  else main_ref[hp, 0, pl.ds(i0 - HALF_WIN, HALF_WIN), :])
            hi = (next_ref[hp, 0, pl.ds(0, HALF_WIN), :] if last
                  else main_ref[hp, 0, pl.ds(i0 + qb, HALF_WIN), :])
            return jnp.concatenate([lo, main_ref[hp, 0, own, :], hi], axis=0).astype(jnp.bfloat16)

        return (0, 1, qb, lambda hp: q_ref[hp, 0, own, :],
                lambda hp: window(kp_ref, km_ref, kn_ref, hp), lambda hp: window(vp_ref, vm_ref, vn_ref, hp),
                own, t0 + i0 - HALF_WIN, first or last)

    nblk = tq // Q_BLOCK
    blocks([dense_desc(0, True, False), dense_desc((nblk - 1) * Q_BLOCK, False, True)])

    def dense_mid(j, _):
        i0 = pl.multiple_of((BLOCK_GROUP * j + 1) * Q_BLOCK, Q_BLOCK)
        blocks([dense_desc(i0 + g * Q_BLOCK, False, False) for g in range(BLOCK_GROUP)])
        return 0

    lax.fori_loop(0, (nblk - 2) // BLOCK_GROUP, dense_mid, 0)

    quarter = tq // 4

    def per_class(r4, _):
        for hp in range(HEAD_PAIRS):
            qd_scr[hp] = q_ref[hp, 0, pl.ds(r4, quarter, stride=4), :]
            for part, (kr, vr) in enumerate(((kp_ref, vp_ref), (km_ref, vm_ref), (kn_ref, vn_ref))):
                dst = pl.ds(part * quarter, quarter)
                kd_scr[hp, dst, :] = kr[hp, 0, pl.ds(r4, quarter, stride=4), :]
                vd_scr[hp, dst, :] = vr[hp, 0, pl.ds(r4, quarter, stride=4), :]

        def desc4(i0):
            win = pl.ds(quarter + i0 - HALF_WIN, Q_BLOCK + 2 * HALF_WIN)
            return (1, 4, Q_BLOCK, lambda hp: qd_scr[hp, pl.ds(i0, Q_BLOCK), :],
                    lambda hp: kd_scr[hp, win, :].astype(jnp.bfloat16),
                    lambda hp: vd_scr[hp, win, :].astype(jnp.bfloat16),
                    pl.ds(r4 + 4 * i0, Q_BLOCK, stride=4), t0 // 4 + i0 - HALF_WIN, True)

        for i0 in range(0, quarter, BLOCK_GROUP * Q_BLOCK):
            blocks([desc4(i0 + g * Q_BLOCK) for g in range(BLOCK_GROUP)])

        n16 = tq // 16

        def desc16(a):
            win = pl.ds(a + quarter - 4 * HALF_WIN, n16 + 2 * HALF_WIN, stride=4)
            return (2, 16, n16, lambda hp: qd_scr[hp, pl.ds(a, n16, stride=4), :],
                    lambda hp: kd_scr[hp, win, :].astype(jnp.bfloat16),
                    lambda hp: vd_scr[hp, win, :].astype(jnp.bfloat16),
                    pl.ds(4 * a + r4, n16, stride=16), t0 // 16 - HALF_WIN, True)

        def per_a(j, _):
            blocks([desc16(BLOCK_GROUP * j + g) for g in range(BLOCK_GROUP)])
            return 0

        lax.fori_loop(0, 4 // BLOCK_GROUP, per_a, 0)
        return 0

    lax.fori_loop(0, 4, per_class, 0)

    outs = []
    for hp in range(HEAD_PAIRS):
        lses = [lse_scr[ip, hp] for ip in range(len(DILATED_PATTERNS))]
        top = functools.reduce(jnp.maximum, lses)
        ws = [jnp.exp(x - top) for x in lses]
        num = sum(w * o_scr[ip, hp] for ip, w in enumerate(ws))
        outs.append(num / sum(ws))
    o = jnp.concatenate(outs, axis=1)
    ms = jnp.mean(o * o, axis=-1, keepdims=True)
    o_ref[0] = (o * lax.rsqrt(ms + NORM_EPS) * g_ref[...]).astype(o_ref.dtype)


def _attn(q, k, v, g_out):
    _, b, s, _ = q.shape
    tq = TQ_ATTN
    nt = s // tq
    npat = len(DILATED_PATTERNS)
    blk = (HEAD_PAIRS, 1, tq, LANES)
    main = pl.BlockSpec(blk, lambda ib, it: (0, ib, it, 0))
    prev = pl.BlockSpec(blk, lambda ib, it: (0, ib, jnp.maximum(it - 1, 0), 0))
    nxt = pl.BlockSpec(blk, lambda ib, it: (0, ib, jnp.minimum(it + 1, nt - 1), 0))
    return pl.pallas_call(
        functools.partial(_attn_kernel, seq_len=s, tq=tq),
        grid=(b, nt),
        in_specs=[main, prev, main, nxt, prev, main, nxt, _const_spec((1, ATTN_WIDTH))],
        out_specs=pl.BlockSpec((1, tq, ATTN_WIDTH), lambda ib, it: (ib, it, 0)),
        out_shape=jax.ShapeDtypeStruct((b, s, ATTN_WIDTH), jnp.bfloat16),
        scratch_shapes=[pltpu.VMEM((npat, HEAD_PAIRS, tq, LANES), jnp.float32),
                        pltpu.VMEM((npat, HEAD_PAIRS, tq, LANES), jnp.float32),
                        pltpu.VMEM((npat, HEAD_PAIRS, 2 * Q_BLOCK, Q_BLOCK + 2 * HALF_WIN), jnp.float32),
                        pltpu.VMEM((HEAD_PAIRS, tq // 4, LANES), jnp.float32),
                        pltpu.VMEM((HEAD_PAIRS, 3 * tq // 4, LANES), jnp.float32),
                        pltpu.VMEM((HEAD_PAIRS, 3 * tq // 4, LANES), jnp.float32),
                        pltpu.VMEM((BLOCK_GROUP * HEAD_PAIRS, 2 * Q_BLOCK, Q_BLOCK + 2 * HALF_WIN), jnp.float32),
                        pltpu.VMEM((BLOCK_GROUP * HEAD_PAIRS, 2 * Q_BLOCK, Q_BLOCK + 2 * HALF_WIN), jnp.bfloat16)],
        compiler_params=_cparams(("parallel", "parallel")),
        name="attn",
    )(q, k, k, k, v, v, v, g_out)


def _outproj_kernel(x_ref, y_ref, b_ref, wglu_ref, bglu_ref, ga_ref, wa_ref, wb_ref, g_ref, x1_ref, n_ref):
    y = jnp.concatenate([y_ref[s] for s in range(S5_SLABS)], axis=1)
    z = jax.nn.gelu(y)
    gate = jax.nn.sigmoid(
        jnp.dot(z.astype(jnp.bfloat16), wglu_ref[...], preferred_element_type=jnp.float32) + bglu_ref[...])
    a = z * gate
    ms = jnp.mean(a * a, axis=-1, keepdims=True)
    a_n = (a * lax.rsqrt(ms + NORM_EPS) * ga_ref[...]).astype(jnp.bfloat16)
    x1 = (x_ref[...]
          + jnp.dot(a_n, wa_ref[...], preferred_element_type=jnp.float32)
          + jnp.dot(b_ref[...], wb_ref[...], preferred_element_type=jnp.float32))
    x1_ref[...] = x1
    ms = jnp.mean(x1 * x1, axis=-1, keepdims=True)
    n_ref[...] = (x1 * lax.rsqrt(ms + NORM_EPS) * g_ref[...]).astype(n_ref.dtype)


def _outproj(x2, y, b_n, w_glu, b_glu, g_a, wa, wb, g):
    t = x2.shape[0]
    tm = TM_PROJ
    row = lambda w: pl.BlockSpec((tm, w), lambda i: (i, 0))
    return pl.pallas_call(
        _outproj_kernel,
        grid=(t // tm,),
        in_specs=[row(D_MODEL), pl.BlockSpec((S5_SLABS, tm, LANES), lambda i: (0, i, 0)), row(ATTN_WIDTH),
                  _const_spec((S5_WIDTH, S5_WIDTH)), _const_spec((1, S5_WIDTH)), _const_spec((1, S5_WIDTH)),
                  _const_spec((S5_WIDTH, D_MODEL)), _const_spec((ATTN_WIDTH, D_MODEL)),
                  _const_spec((1, D_MODEL))],
        out_specs=[row(D_MODEL), row(D_MODEL)],
        out_shape=[jax.ShapeDtypeStruct((t, D_MODEL), jnp.float32),
                   jax.ShapeDtypeStruct((t, D_MODEL), jnp.bfloat16)],
        compiler_params=_cparams(("parallel",)),
        name="outproj",
    )(x2, y, b_n, w_glu, b_glu, g_a, wa, wb, g)


def _ffn_kernel(nm_ref, np_ref, nn_ref, x1_ref, wup_ref, cw_ref, cb_ref, wdn_ref, o_ref, h_scr, act_scr, *,
                tiles_per_seq, tm):
    pos = pl.program_id(0) % tiles_per_seq
    halo = BF16_ROWS
    prev = jnp.where(pos == 0, jnp.zeros_like(np_ref[...]), np_ref[...])
    nxt = jnp.where(pos == tiles_per_seq - 1, jnp.zeros_like(nn_ref[...]), nn_ref[...])
    nh = jnp.concatenate([prev, nm_ref[...], nxt], axis=0)
    slabs = TF_FFN // LANES

    def conv(h, off, base):
        outs = []
        for s in range(slabs):
            h_scr[base + s] = h[:, s * LANES:(s + 1) * LANES]
            cols = pl.ds(off + s * LANES, LANES)
            outs.append(h_scr[base + s, pl.ds(halo - 1, tm), :] * cw_ref[0:1, cols]
                        + h_scr[base + s, pl.ds(halo, tm), :] * cw_ref[1:2, cols]
                        + h_scr[base + s, pl.ds(halo + 1, tm), :] * cw_ref[2:3, cols]
                        + cb_ref[:, cols])
        return jnp.concatenate(outs, axis=1)

    for j in range(D_FF // TF_FFN):
        off = j * TF_FFN
        base = (j % 2) * 2 * slabs
        hg = jnp.dot(nh, wup_ref[:, pl.ds(off, TF_FFN)], preferred_element_type=jnp.float32)
        hu = jnp.dot(nh, wup_ref[:, pl.ds(D_FF + off, TF_FFN)], preferred_element_type=jnp.float32)
        g = conv(hg, off, base)
        up = conv(hu, D_FF + off, base + slabs)
        act_scr[:, pl.ds(off, TF_FFN)] = (g * jax.nn.sigmoid(g) * up).astype(jnp.bfloat16)
    o_ref[...] = x1_ref[...] + jnp.dot(act_scr[...], wdn_ref[...], preferred_element_type=jnp.float32)


def _ffn(n2, x1, w_up, conv_w, conv_b, w_down, seq_len):
    t = n2.shape[0]
    tm = TM_FFN
    halo = BF16_ROWS
    hb = tm // halo
    nhb = t // halo
    row = lambda w: pl.BlockSpec((tm, w), lambda i: (i, 0))
    prev = pl.BlockSpec((halo, D_MODEL), lambda i: (jnp.maximum(i * hb - 1, 0), 0))
    nxt = pl.BlockSpec((halo, D_MODEL), lambda i: (jnp.minimum((i + 1) * hb, nhb - 1), 0))
    return pl.pallas_call(
        functools.partial(_ffn_kernel, tiles_per_seq=seq_len // tm, tm=tm),
        grid=(t // tm,),
        in_specs=[row(D_MODEL), prev, nxt, row(D_MODEL), _const_spec((D_MODEL, 2 * D_FF)),
                  _const_spec((3, 2 * D_FF)), _const_spec((1, 2 * D_FF)), _const_spec((D_FF, D_MODEL))],
        out_specs=row(D_MODEL),
        out_shape=jax.ShapeDtypeStruct((t, D_MODEL), jnp.float32),
        scratch_shapes=[pltpu.VMEM((4 * TF_FFN // LANES, tm + 2 * halo, LANES), jnp.float32),
                        pltpu.VMEM((tm, D_FF), jnp.bfloat16)],
        compiler_params=_cparams(("parallel",)),
        name="ffn",
    )(n2, n2, n2, x1, w_up, conv_w, conv_b, w_down)


def _layer(x, p):
    b, s, d = x.shape
    x2 = x.reshape(b * s, d)
    u, q, k, v = _proj(x2, p["norm_mix_g"], p["w_in"], p["gq"], p["gk"])
    shp = (HEAD_PAIRS, b, s, LANES)
    y = _s5(u.reshape(shp), p["s5_mats"])
    b_n = _attn(q.reshape(shp), k.reshape(shp), v.reshape(shp), p["attn_out_g"])
    x1, n2 = _outproj(x2, y.reshape(S5_SLABS, b * s, LANES), b_n.reshape(b * s, ATTN_WIDTH),
                      p["w_glu"], p["b_glu"], p["ssm_out_g"], p["w_out_a"], p["w_out_b"], p["norm_ffn_g"])
    out = _ffn(n2, x1, p["w_up"], p["conv_w"], p["conv_b"], p["w_down"], s)
    return out.reshape(b, s, d)


def kernel(x_prompt, x_sample, norm_mix_g, w_in, s5_a_re, s5_a_im, s5_log_dt, s5_b_re, s5_b_im, s5_c_re, s5_c_im, s5_d, w_glu, b_glu, q_norm_g, k_norm_g, ssm_out_g, attn_out_g, w_out, norm_ffn_g, w_up, conv_w, conv_b, w_down):
    depth = w_in.shape[0]
    f32, bf16 = jnp.float32, jnp.bfloat16
    y_prompt, y_sample = x_prompt, x_sample
    for i in range(depth):
        p = {
            "norm_mix_g": norm_mix_g[i].astype(f32)[None],
            "w_in": w_in[i].astype(bf16),
            "gq": jnp.tile(q_norm_g[i].astype(f32), N_HEADS)[None],
            "gk": jnp.tile(k_norm_g[i].astype(f32), N_HEADS)[None],
            "s5_mats": _s5_matrices(s5_a_re[i], s5_a_im[i], s5_log_dt[i], s5_b_re[i], s5_b_im[i],
                                    s5_c_re[i], s5_c_im[i], s5_d[i]),
            "w_glu": w_glu[i].astype(bf16),
            "b_glu": b_glu[i].astype(f32)[None],
            "ssm_out_g": ssm_out_g[i].astype(f32)[None],
            "attn_out_g": attn_out_g[i].astype(f32)[None],
            "w_out_a": w_out[i, :S5_WIDTH].astype(bf16),
            "w_out_b": w_out[i, S5_WIDTH:].astype(bf16),
            "norm_ffn_g": norm_ffn_g[i].astype(f32)[None],
            "w_up": w_up[i].astype(bf16),
            "conv_w": conv_w[i].astype(f32),
            "conv_b": conv_b[i].astype(f32)[None],
            "w_down": w_down[i].astype(bf16),
        }
        y_prompt = _layer(y_prompt, p)
        y_sample = _layer(y_sample, p)
    return (y_prompt, y_sample)
```

```python
import functools

import jax
import jax.numpy as jnp
from jax import lax
from jax.experimental import pallas as pl
from jax.experimental.pallas import tpu as pltpu

D_MODEL = 1024
S5_WIDTH = 512
S5_CH = 16
S5_GROUPS = 32
S5_STATE = 64
ATTN_WIDTH = 512
HEAD_DIM = 64
N_HEADS = 8
HEAD_PAIRS = N_HEADS // 2
DILATED_PATTERNS = ((128, 1), (512, 4), (2048, 16))
D_IN = S5_WIDTH + 3 * ATTN_WIDTH
D_FF = 2816
NORM_EPS = 1e-6
NEG_INF = -1e30

LANES = 128
BF16_ROWS = 16
VMEM_LIMIT = 56 * 1024 * 1024

S5_SLABS = S5_WIDTH // LANES
SLAB_GROUPS = LANES // S5_CH
S5_CHUNK = 16
S5_BLOCK = S5_CHUNK * S5_CH
MAX_SCAN_STEPS = 10
S5_LANES = 1024

TM_PROJ = 512
TQ_ATTN = 1024
Q_BLOCK = 128
BLOCK_GROUP = 2
HALF_WIN = 64
TM_FFN = 512
TF_FFN = 256

_NT = (((1,), (1,)), ((), ()))


def _cparams(sem):
    return pltpu.CompilerParams(dimension_semantics=sem, vmem_limit_bytes=VMEM_LIMIT)


def _const_spec(shape):
    nd = len(shape)
    return pl.BlockSpec(shape, lambda *_: (0,) * nd)


def _proj_kernel(x_ref, g_ref, w_ref, gq_ref, gk_ref, u_ref, q_ref, k_ref, v_ref):
    x = x_ref[...]
    ms = jnp.mean(x * x, axis=-1, keepdims=True)
    n = (x * lax.rsqrt(ms + NORM_EPS) * g_ref[...]).astype(jnp.bfloat16)
    proj = jnp.dot(n, w_ref[...], preferred_element_type=jnp.float32)

    first_head = lax.broadcasted_iota(jnp.int32, (1, LANES), 1) < HEAD_DIM

    def head_norm(t, gain):
        sq = t * t
        tots = []
        for s in range(HEAD_PAIRS):
            blk = sq[:, s * LANES:(s + 1) * LANES]
            first = jnp.sum(jnp.where(first_head, blk, 0.0), axis=-1, keepdims=True)
            second = jnp.sum(jnp.where(first_head, 0.0, blk), axis=-1, keepdims=True)
            tots.append(jnp.where(first_head, first, second))
        tot = jnp.concatenate(tots, axis=1)
        return t * lax.rsqrt(tot * (1.0 / HEAD_DIM) + NORM_EPS) * gain

    u = proj[:, :S5_WIDTH]
    q = proj[:, S5_WIDTH:S5_WIDTH + ATTN_WIDTH]
    k = proj[:, S5_WIDTH + ATTN_WIDTH:S5_WIDTH + 2 * ATTN_WIDTH]
    q = head_norm(q, gq_ref[...]) * (HEAD_DIM ** -0.5)
    k = head_norm(k, gk_ref[...])
    v = proj[:, S5_WIDTH + 2 * ATTN_WIDTH:]
    for s in range(S5_SLABS):
        cols = slice(s * LANES, (s + 1) * LANES)
        u_ref[s] = u[:, cols]
        q_ref[s] = q[:, cols]
        k_ref[s] = k[:, cols]
        v_ref[s] = v[:, cols]


def _proj(x2, g, w_in, gq, gk):
    t = x2.shape[0]
    tm = TM_PROJ
    slab = pl.BlockSpec((HEAD_PAIRS, tm, LANES), lambda i: (0, i, 0))
    out = jax.ShapeDtypeStruct((HEAD_PAIRS, t, LANES), jnp.float32)
    return pl.pallas_call(
        _proj_kernel,
        grid=(t // tm,),
        in_specs=[pl.BlockSpec((tm, D_MODEL), lambda i: (i, 0)), _const_spec((1, D_MODEL)),
                  _const_spec((D_MODEL, D_IN)), _const_spec((1, ATTN_WIDTH)), _const_spec((1, ATTN_WIDTH))],
        out_specs=[slab, slab, slab, slab],
        out_shape=[out, out, out, out],
        compiler_params=_cparams(("parallel",)),
        name="proj",
    )(x2, g, w_in, gq, gk)


def _chunk_scan(xr, xi, tab_ref, gi, col0, pos, n_chunks, reverse):
    lanes = xr.shape[1]
    k, step = 1, 0
    while k < n_chunks:
        mr = tab_ref[gi, :, col0 + 2 * step:col0 + 2 * step + 1]
        mi = tab_ref[gi, :, col0 + 2 * step + 1:col0 + 2 * step + 2]
        keep = (pos < n_chunks - k) if reverse else (pos >= k)
        shift = (lanes - k) if reverse else k
        sr = jnp.where(keep, pltpu.roll(xr, shift, 1), 0.0)
        si = jnp.where(keep, pltpu.roll(xi, shift, 1), 0.0)
        xr, xi = xr + (mr * sr - mi * si), xi + (mr * si + mi * sr)
        k, step = 2 * k, step + 1
    return xr, xi


def _s5_kernel(u_ref, toep_ref, p_ref, q_ref, tab_ref, y_ref, xt_scr, yt_scr, *, n_chunks, n_seq):
    n = n_chunks
    lanes = n_seq * n
    for b in range(n_seq):
        for t in range(S5_CHUNK):
            rows = pl.ds(t, n, stride=S5_CHUNK)
            xt_scr[t, :, b * n:(b + 1) * n] = u_ref[0, b, rows, :].T.astype(jnp.bfloat16)
    pos = lax.broadcasted_iota(jnp.int32, (1, lanes), 1) % n
    ns = S5_STATE

    def per_group(gi, _):
        ch = pl.ds(pl.multiple_of(gi * S5_CH, S5_CH), S5_CH)
        x = jnp.concatenate([xt_scr[t, ch, :] for t in range(S5_CHUNK)], axis=0)
        y = jnp.dot(toep_ref[gi], x, preferred_element_type=jnp.float32)
        st = jnp.dot(p_ref[gi], x, preferred_element_type=jnp.float32)
        fr, fi = _chunk_scan(st[0:ns], st[ns:2 * ns], tab_ref, gi, 0, pos, n, False)
        br, bi = _chunk_scan(st[2 * ns:3 * ns], st[3 * ns:4 * ns], tab_ref, gi, 2 * MAX_SCAN_STEPS, pos, n, True)
        prev = lambda a: jnp.where(pos >= 1, pltpu.roll(a, 1, 1), 0.0)
        nxt = lambda a: jnp.where(pos < n - 1, pltpu.roll(a, lanes - 1, 1), 0.0)
        h = jnp.concatenate([prev(fr), prev(fi), nxt(br), nxt(bi)], axis=0).astype(jnp.bfloat16)
        y = y + jnp.dot(q_ref[gi], h, preferred_element_type=jnp.float32)
        for t in range(S5_CHUNK):
            yt_scr[t, ch, :] = y[t * S5_CH:(t + 1) * S5_CH, :]
        return 0

    lax.fori_loop(0, SLAB_GROUPS, per_group, 0)
    for b in range(n_seq):
        for t in range(S5_CHUNK):
            y_ref[0, b, pl.ds(t, n, stride=S5_CHUNK), :] = yt_scr[t, :, b * n:(b + 1) * n].T


def _s5(u, mats):
    _, b, s, _ = u.shape
    n = s // S5_CHUNK
    nb = max(1, min(b, S5_LANES // n))
    toep, pmat, qmat, tab = mats
    io = pl.BlockSpec((1, nb, s, LANES), lambda isl, ib: (isl, ib, 0, 0))
    wspec = pl.BlockSpec((SLAB_GROUPS, S5_BLOCK, S5_BLOCK), lambda isl, ib: (isl, 0, 0))
    return pl.pallas_call(
        functools.partial(_s5_kernel, n_chunks=n, n_seq=nb),
        grid=(S5_SLABS, b // nb),
        in_specs=[io, wspec, wspec, wspec,
                  pl.BlockSpec((SLAB_GROUPS, S5_STATE, LANES), lambda isl, ib: (isl, 0, 0))],
        out_specs=io,
        out_shape=jax.ShapeDtypeStruct(u.shape, jnp.float32),
        scratch_shapes=[pltpu.VMEM((S5_CHUNK, LANES, nb * n), jnp.bfloat16),
                        pltpu.VMEM((S5_CHUNK, LANES, nb * n), jnp.float32)],
        compiler_params=_cparams(("parallel", "parallel")),
        name="s5",
    )(u, toep, pmat, qmat, tab)


def _s5_matrices(a_re, a_im, log_dt, b_re, b_im, c_re, c_im, d_skip):
    f32 = jnp.float32
    hp = lax.Precision.HIGHEST
    n, g_, p_ = S5_CHUNK, S5_GROUPS, S5_STATE
    cmul = lambda a, b: (a[0] * b[0] - a[1] * b[1], a[0] * b[1] + a[1] * b[0])
    kern, p_rows, q_cols, tab_cols = [], [], [], []
    for direction in range(2):
        ar = a_re[direction].astype(f32)
        ai = a_im[direction].astype(f32)
        dt = jnp.exp(log_dt[direction].astype(f32))[:, None]
        mag = jnp.exp(ar * dt)
        lam = (mag * jnp.cos(ai * dt), mag * jnp.sin(ai * dt))
        den = ar * ar + ai * ai
        nr = lam[0] - 1.0
        fr = (nr * ar + lam[1] * ai) / den
        fi = (lam[1] * ar - nr * ai) / den
        br = b_re[direction].astype(f32)
        bi = b_im[direction].astype(f32)
        bbar = (fr[:, :, None] * br - fi[:, :, None] * bi, fr[:, :, None] * bi + fi[:, :, None] * br)
        cr = c_re[direction].astype(f32)
        ci = c_im[direction].astype(f32)
        powers = [(jnp.ones_like(lam[0]), jnp.zeros_like(lam[0]))]
        for _ in range(n):
            powers.append(cmul(powers[-1], lam))
        ks = list(range(n - 1, -1, -1)) if direction == 0 else list(range(n))
        pw = (jnp.stack([powers[k][0] for k in ks], axis=-1)[..., None],
              jnp.stack([powers[k][1] for k in ks], axis=-1)[..., None])
        m = cmul(pw, (bbar[0][:, :, None, :], bbar[1][:, :, None, :]))
        m = (m[0].reshape(g_, p_, S5_BLOCK), m[1].reshape(g_, p_, S5_BLOCK))
        p_rows += [m[0], m[1]]
        kern.append(jnp.einsum('gap,gpx->gax', cr, m[0], precision=hp)
                    - jnp.einsum('gap,gpx->gax', ci, m[1], precision=hp))
        ts = [t + 1 for t in range(n)] if direction == 0 else [n - t for t in range(n)]
        lt = (jnp.stack([powers[k][0] for k in ts], axis=1)[:, :, None, :],
              jnp.stack([powers[k][1] for k in ts], axis=1)[:, :, None, :])
        z = cmul((cr[:, None], ci[:, None]), lt)
        q_cols += [z[0].reshape(g_, S5_BLOCK, p_), -z[1].reshape(g_, S5_BLOCK, p_)]
        step = powers[n]
        for _ in range(MAX_SCAN_STEPS):
            tab_cols += [step[0], step[1]]
            step = cmul(step, step)
    last = (n - 1) * S5_CH
    eye = jnp.eye(S5_CH, dtype=f32)
    centre = kern[0][:, :, last:] + kern[1][:, :, :S5_CH] + d_skip.astype(f32).reshape(g_, S5_CH)[:, :, None] * eye
    by_lag = jnp.concatenate([kern[0][:, :, :last], centre, kern[1][:, :, S5_CH:]], axis=-1)
    toep = jnp.stack([by_lag[:, :, (n - 1 - t) * S5_CH:(n - 1 - t) * S5_CH + S5_BLOCK] for t in range(n)],
                     axis=1).reshape(g_, S5_BLOCK, S5_BLOCK)
    bf16 = jnp.bfloat16
    pmat = jnp.concatenate(p_rows, axis=1)
    qmat = jnp.concatenate(q_cols, axis=2)
    pad = jnp.zeros((g_, p_, LANES - len(tab_cols)), f32)
    tab = jnp.concatenate([jnp.stack(tab_cols, axis=-1), pad], axis=-1)
    return toep.astype(bf16), pmat.astype(bf16), qmat.astype(bf16), tab


def _attn_kernel(q_ref, kp_ref, km_ref, kn_ref, vp_ref, vm_ref, vn_ref, g_ref, o_ref,
                 o_scr, lse_scr, bias_scr, qd_scr, kd_scr, vd_scr, s_scr, p_scr, *, seq_len, tq):
    t0 = pl.program_id(1) * tq
    lane = lax.broadcasted_iota(jnp.int32, (1, LANES), 1)
    first_head = lane < HEAD_DIM

    for ip, (_, dil) in enumerate(DILATED_PATTERNS):
        qb = min(Q_BLOCK, tq // dil)
        width = qb + 2 * HALF_WIN
        qi = lax.broadcasted_iota(jnp.int32, (qb, width), 0)
        kj = lax.broadcasted_iota(jnp.int32, (qb, width), 1) - HALF_WIN
        rel = jnp.abs(qi - kj)
        dist = (dil * rel).astype(jnp.float32)
        for h in range(N_HEADS):
            slope = 2.0 ** (-8.0 * (h + 1) / N_HEADS)
            bias_scr[ip, h // 2, (h % 2) * qb:(h % 2 + 1) * qb, 0:width] = jnp.where(
                rel <= HALF_WIN, -slope * dist, NEG_INF)

    def blocks(descs):
        chains = [(bi, hp) for bi in range(len(descs)) for hp in range(HEAD_PAIRS)]
        for c, (bi, hp) in enumerate(chains):
            ip, dil, qb, q_of, k_of, _, _, first_key, check_ends = descs[bi]
            width = qb + 2 * HALF_WIN
            q2 = q_of(hp)
            k2 = k_of(hp)
            qq = jnp.concatenate([jnp.where(first_head, q2, 0.0), jnp.where(first_head, 0.0, q2)],
                                 axis=0).astype(jnp.bfloat16)
            s = lax.dot_general(qq, k2, _NT, preferred_element_type=jnp.float32)
            s = s + bias_scr[ip, hp, 0:2 * qb, 0:width]
            if check_ends:
                kpos = lax.broadcasted_iota(jnp.int32, (1, width), 1) + first_key
                s = jnp.where((kpos >= 0) & (kpos < seq_len // dil), s, NEG_INF)
            s_scr[c, 0:2 * qb, 0:width] = s
        stats = []
        for c, (bi, hp) in enumerate(chains):
            qb = descs[bi][2]
            width = qb + 2 * HALF_WIN
            s = s_scr[c, 0:2 * qb, 0:width]
            m = jnp.max(s, axis=-1, keepdims=True)
            p = jnp.exp(s - m)
            l = jnp.sum(p, axis=-1, keepdims=True)
            p_scr[c, 0:2 * qb, 0:width] = p.astype(jnp.bfloat16)
            stats.append((1.0 / l, m + jnp.log(l)))
        for c, (bi, hp) in enumerate(chains):
            ip, _, qb, _, _, v_of, out_rows, _, _ = descs[bi]
            width = qb + 2 * HALF_WIN
            inv_l, lse = stats[c]
            o2 = jnp.dot(p_scr[c, 0:2 * qb, 0:width], v_of(hp), preferred_element_type=jnp.float32)
            o2 = o2 * inv_l
            o_scr[ip, hp, out_rows, :] = jnp.where(first_head, o2[:qb], o2[qb:])
            lse_scr[ip, hp, out_rows, :] = jnp.where(first_head, lse[:qb], lse[qb:])

    def dense_desc(i0, first, last):
        qb = Q_BLOCK
        own = pl.ds(i0, qb)

        def window(prev_ref, main_ref, next_ref, hp):
            lo = (prev_ref[hp, 0, pl.ds(tq - HALF_WIN, HALF_WIN), :] if first
                  else main_ref[hp, 0, pl.ds(i0 - HALF_WIN, HALF_WIN), :])
            hi = (next_ref[hp, 0, pl.ds(0, HALF_WIN), :] if last
                  else main_ref[hp, 0, pl.ds(i0 + qb, HALF_WIN), :])
            return jnp.concatenate([lo, main_ref[hp, 0, own, :], hi], axis=0).astype(jnp.bfloat16)

        return (0, 1, qb, lambda hp: q_ref[hp, 0, own, :],
                lambda hp: window(kp_ref, km_ref, kn_ref, hp), lambda hp: window(vp_ref, vm_ref, vn_ref, hp),
                own, t0 + i0 - HALF_WIN, first or last)

    nblk = tq // Q_BLOCK
    blocks([dense_desc(0, True, False), dense_desc((nblk - 1) * Q_BLOCK, False, True)])

    def dense_mid(j, _):
        i0 = pl.multiple_of((BLOCK_GROUP * j + 1) * Q_BLOCK, Q_BLOCK)
        blocks([dense_desc(i0 + g * Q_BLOCK, False, False) for g in range(BLOCK_GROUP)])
        return 0

    lax.fori_loop(0, (nblk - 2) // BLOCK_GROUP, dense_mid, 0)

    quarter = tq // 4

    def per_class(r4, _):
        for hp in range(HEAD_PAIRS):
            qd_scr[hp] = q_ref[hp, 0, pl.ds(r4, quarter, stride=4), :]
            for part, (kr, vr) in enumerate(((kp_ref, vp_ref), (km_ref, vm_ref), (kn_ref, vn_ref))):
                dst = pl.ds(part * quarter, quarter)
                kd_scr[hp, dst, :] = kr[hp, 0, pl.ds(r4, quarter, stride=4), :]
                vd_scr[hp, dst, :] = vr[hp, 0, pl.ds(r4, quarter, stride=4), :]

        def desc4(i0):
            win = pl.ds(quarter + i0 - HALF_WIN, Q_BLOCK + 2 * HALF_WIN)
            return (1, 4, Q_BLOCK, lambda hp: qd_scr[hp, pl.ds(i0, Q_BLOCK), :],
                    lambda hp: kd_scr[hp, win, :].astype(jnp.bfloat16),
                    lambda hp: vd_scr[hp, win, :].astype(jnp.bfloat16),
                    pl.ds(r4 + 4 * i0, Q_BLOCK, stride=4), t0 // 4 + i0 - HALF_WIN, True)

        for i0 in range(0, quarter, BLOCK_GROUP * Q_BLOCK):
            blocks([desc4(i0 + g * Q_BLOCK) for g in range(BLOCK_GROUP)])

        n16 = tq // 16

        def desc16(a):
            win = pl.ds(a + quarter - 4 * HALF_WIN, n16 + 2 * HALF_WIN, stride=4)
            return (2, 16, n16, lambda hp: qd_scr[hp, pl.ds(a, n16, stride=4), :],
                    lambda hp: kd_scr[hp, win, :].astype(jnp.bfloat16),
                    lambda hp: vd_scr[hp, win, :].astype(jnp.bfloat16),
                    pl.ds(4 * a + r4, n16, stride=16), t0 // 16 - HALF_WIN, True)

        def per_a(j, _):
            blocks([desc16(BLOCK_GROUP * j + g) for g in range(BLOCK_GROUP)])
            return 0

        lax.fori_loop(0, 4 // BLOCK_GROUP, per_a, 0)
        return 0

    lax.fori_loop(0, 4, per_class, 0)

    outs = []
    for hp in range(HEAD_PAIRS):
        lses = [lse_scr[ip, hp] for ip in range(len(DILATED_PATTERNS))]
        top = functools.reduce(jnp.maximum, lses)
        ws = [jnp.exp(x - top) for x in lses]
        num = sum(w * o_scr[ip, hp] for ip, w in enumerate(ws))
        outs.append(num / sum(ws))
    o = jnp.concatenate(outs, axis=1)
    ms = jnp.mean(o * o, axis=-1, keepdims=True)
    o_ref[0] = (o * lax.rsqrt(ms + NORM_EPS) * g_ref[...]).astype(o_ref.dtype)


def _attn(q, k, v, g_out):
    _, b, s, _ = q.shape
    tq = TQ_ATTN
    nt = s // tq
    npat = len(DILATED_PATTERNS)
    blk = (HEAD_PAIRS, 1, tq, LANES)
    main = pl.BlockSpec(blk, lambda ib, it: (0, ib, it, 0))
    prev = pl.BlockSpec(blk, lambda ib, it: (0, ib, jnp.maximum(it - 1, 0), 0))
    nxt = pl.BlockSpec(blk, lambda ib, it: (0, ib, jnp.minimum(it + 1, nt - 1), 0))
    return pl.pallas_call(
        functools.partial(_attn_kernel, seq_len=s, tq=tq),
        grid=(b, nt),
        in_specs=[main, prev, main, nxt, prev, main, nxt, _const_spec((1, ATTN_WIDTH))],
        out_specs=pl.BlockSpec((1, tq, ATTN_WIDTH), lambda ib, it: (ib, it, 0)),
        out_shape=jax.ShapeDtypeStruct((b, s, ATTN_WIDTH), jnp.bfloat16),
        scratch_shapes=[pltpu.VMEM((npat, HEAD_PAIRS, tq, LANES), jnp.float32),
                        pltpu.VMEM((npat, HEAD_PAIRS, tq, LANES), jnp.float32),
                        pltpu.VMEM((npat, HEAD_PAIRS, 2 * Q_BLOCK, Q_BLOCK + 2 * HALF_WIN), jnp.float32),
                        pltpu.VMEM((HEAD_PAIRS, tq // 4, LANES), jnp.float32),
                        pltpu.VMEM((HEAD_PAIRS, 3 * tq // 4, LANES), jnp.float32),
                        pltpu.VMEM((HEAD_PAIRS, 3 * tq // 4, LANES), jnp.float32),
                        pltpu.VMEM((BLOCK_GROUP * HEAD_PAIRS, 2 * Q_BLOCK, Q_BLOCK + 2 * HALF_WIN), jnp.float32),
                        pltpu.VMEM((BLOCK_GROUP * HEAD_PAIRS, 2 * Q_BLOCK, Q_BLOCK + 2 * HALF_WIN), jnp.bfloat16)],
        compiler_params=_cparams(("parallel", "parallel")),
        name="attn",
    )(q, k, k, k, v, v, v, g_out)


def _mix_ffn_kernel(xm_ref, xp_ref, xn_ref, ym_ref, yp_ref, yn_ref, bm_ref, bp_ref, bn_ref,
                    wglu_ref, bglu_ref, ga_ref, wa_ref, wb_ref, gf_ref, wup_ref, cw_ref, cb_ref, wdn_ref,
                    o_ref, h_scr, act_scr, *, tiles_per_seq, tm):
    pos = pl.program_id(0) % tiles_per_seq
    halo = BF16_ROWS
    rows = tm + 2 * halo
    x = jnp.concatenate([xp_ref[...], xm_ref[...], xn_ref[...]], axis=0)
    y = jnp.concatenate(
        [jnp.concatenate([yp_ref[s], ym_ref[s], yn_ref[s]], axis=0) for s in range(S5_SLABS)], axis=1)
    b_n = jnp.concatenate([bp_ref[...], bm_ref[...], bn_ref[...]], axis=0)
    z = jax.nn.gelu(y)
    gate = jax.nn.sigmoid(
        jnp.dot(z.astype(jnp.bfloat16), wglu_ref[...], preferred_element_type=jnp.float32) + bglu_ref[...])
    a = z * gate
    ms = jnp.mean(a * a, axis=-1, keepdims=True)
    a_n = (a * lax.rsqrt(ms + NORM_EPS) * ga_ref[...]).astype(jnp.bfloat16)
    x1 = (x + jnp.dot(a_n, wa_ref[...], preferred_element_type=jnp.float32)
          + jnp.dot(b_n, wb_ref[...], preferred_element_type=jnp.float32))
    ms = jnp.mean(x1 * x1, axis=-1, keepdims=True)
    n = x1 * lax.rsqrt(ms + NORM_EPS) * gf_ref[...]
    r = lax.broadcasted_iota(jnp.int32, (rows, 1), 0)
    outside = ((r < halo) & (pos == 0)) | ((r >= halo + tm) & (pos == tiles_per_seq - 1))
    nh = jnp.where(outside, 0.0, n).astype(jnp.bfloat16)
    slabs = TF_FFN // LANES

    def conv(h, off, base):
        outs = []
        for s in range(slabs):
            h_scr[base + s] = h[:, s * LANES:(s + 1) * LANES]
            cols = pl.ds(off + s * LANES, LANES)
            outs.append(h_scr[base + s, pl.ds(halo - 1, tm), :] * cw_ref[0:1, cols]
                        + h_scr[base + s, pl.ds(halo, tm), :] * cw_ref[1:2, cols]
                        + h_scr[base + s, pl.ds(halo + 1, tm), :] * cw_ref[2:3, cols]
                        + cb_ref[:, cols])
        return jnp.concatenate(outs, axis=1)

    for j in range(D_FF // TF_FFN):
        off = j * TF_FFN
        base = (j % 2) * 2 * slabs
        hg = jnp.dot(nh, wup_ref[:, pl.ds(off, TF_FFN)], preferred_element_type=jnp.float32)
        hu = jnp.dot(nh, wup_ref[:, pl.ds(D_FF + off, TF_FFN)], preferred_element_type=jnp.float32)
        g = conv(hg, off, base)
        up = conv(hu, D_FF + off, base + slabs)
        act_scr[:, pl.ds(off, TF_FFN)] = (g * jax.nn.sigmoid(g) * up).astype(jnp.bfloat16)
    o_ref[...] = x1[halo:halo + tm] + jnp.dot(act_scr[...], wdn_ref[...], preferred_element_type=jnp.float32)


def _mix_ffn(x2, y, b_n, p, seq_len):
    t = x2.shape[0]
    tm = TM_FFN
    halo = BF16_ROWS
    hb = tm // halo
    nhb = t // halo
    prev_i = lambda i: jnp.maximum(i * hb - 1, 0)
    next_i = lambda i: jnp.minimum((i + 1) * hb, nhb - 1)
    row = lambda w: pl.BlockSpec((tm, w), lambda i: (i, 0))
    rowp = lambda w: pl.BlockSpec((halo, w), lambda i: (prev_i(i), 0))
    rown = lambda w: pl.BlockSpec((halo, w), lambda i: (next_i(i), 0))
    slab = lambda r, f: pl.BlockSpec((S5_SLABS, r, LANES), lambda i: (0, f(i), 0))
    once = lambda shape: pl.BlockSpec(shape, lambda i: (0,) * len(shape), pipeline_mode=pl.Buffered(1))
    return pl.pallas_call(
        functools.partial(_mix_ffn_kernel, tiles_per_seq=seq_len // tm, tm=tm),
        grid=(t // tm,),
        in_specs=[row(D_MODEL), rowp(D_MODEL), rown(D_MODEL),
                  slab(tm, lambda i: i), slab(halo, prev_i), slab(halo, next_i),
                  row(ATTN_WIDTH), rowp(ATTN_WIDTH), rown(ATTN_WIDTH),
                  once((S5_WIDTH, S5_WIDTH)), once((1, S5_WIDTH)), once((1, S5_WIDTH)),
                  once((S5_WIDTH, D_MODEL)), once((ATTN_WIDTH, D_MODEL)), once((1, D_MODEL)),
                  once((D_MODEL, 2 * D_FF)), once((3, 2 * D_FF)), once((1, 2 * D_FF)), once((D_FF, D_MODEL))],
        out_specs=row(D_MODEL),
        out_shape=jax.ShapeDtypeStruct((t, D_MODEL), jnp.float32),
        scratch_shapes=[pltpu.VMEM((4 * TF_FFN // LANES, tm + 2 * halo, LANES), jnp.float32),
                        pltpu.VMEM((tm, D_FF), jnp.bfloat16)],
        compiler_params=_cparams(("parallel",)),
        name="mix_ffn",
    )(x2, x2, x2, y, y, y, b_n, b_n, b_n, p["w_glu"], p["b_glu"], p["ssm_out_g"], p["w_out_a"], p["w_out_b"],
      p["norm_ffn_g"], p["w_up"], p["conv_w"], p["conv_b"], p["w_down"])


def _layer(x, p):
    b, s, d = x.shape
    x2 = x.reshape(b * s, d)
    u, q, k, v = _proj(x2, p["norm_mix_g"], p["w_in"], p["gq"], p["gk"])
    shp = (HEAD_PAIRS, b, s, LANES)
    y = _s5(u.reshape(shp), p["s5_mats"])
    b_n = _attn(q.reshape(shp), k.reshape(shp), v.reshape(shp), p["attn_out_g"])
    out = _mix_ffn(x2, y.reshape(S5_SLABS, b * s, LANES), b_n.reshape(b * s, ATTN_WIDTH), p, s)
    return out.reshape(b, s, d)


def kernel(x_prompt, x_sample, norm_mix_g, w_in, s5_a_re, s5_a_im, s5_log_dt, s5_b_re, s5_b_im, s5_c_re, s5_c_im, s5_d, w_glu, b_glu, q_norm_g, k_norm_g, ssm_out_g, attn_out_g, w_out, norm_ffn_g, w_up, conv_w, conv_b, w_down):
    depth = w_in.shape[0]
    f32, bf16 = jnp.float32, jnp.bfloat16
    y_prompt, y_sample = x_prompt, x_sample
    for i in range(depth):
        p = {
            "norm_mix_g": norm_mix_g[i].astype(f32)[None],
            "w_in": w_in[i].astype(bf16),
            "gq": jnp.tile(q_norm_g[i].astype(f32), N_HEADS)[None],
            "gk": jnp.tile(k_norm_g[i].astype(f32), N_HEADS)[None],
            "s5_mats": _s5_matrices(s5_a_re[i], s5_a_im[i], s5_log_dt[i], s5_b_re[i], s5_b_im[i],
                                    s5_c_re[i], s5_c_im[i], s5_d[i]),
            "w_glu": w_glu[i].astype(bf16),
            "b_glu": b_glu[i].astype(f32)[None],
            "ssm_out_g": ssm_out_g[i].astype(f32)[None],
            "attn_out_g": attn_out_g[i].astype(f32)[None],
            "w_out_a": w_out[i, :S5_WIDTH].astype(bf16),
            "w_out_b": w_out[i, S5_WIDTH:].astype(bf16),
            "norm_ffn_g": norm_ffn_g[i].astype(f32)[None],
            "w_up": w_up[i].astype(bf16),
            "conv_w": conv_w[i].astype(f32),
            "conv_b": conv_b[i].astype(f32)[None],
            "w_down": w_down[i].astype(bf16),
        }
        y_prompt = _layer(y_prompt, p)
        y_sample = _layer(y_sample, p)
    return (y_prompt, y_sample)
```

```python
import functools

import jax
import jax.numpy as jnp
from jax import lax
from jax.experimental import pallas as pl
from jax.experimental.pallas import tpu as pltpu

D_MODEL = 1024
S5_WIDTH = 512
S5_CH = 16
S5_GROUPS = 32
S5_STATE = 64
ATTN_WIDTH = 512
HEAD_DIM = 64
N_HEADS = 8
HEAD_PAIRS = N_HEADS // 2
DILATED_PATTERNS = ((128, 1), (512, 4), (2048, 16))
D_IN = S5_WIDTH + 3 * ATTN_WIDTH
D_FF = 2816
NORM_EPS = 1e-6
NEG_INF = -1e30

LANES = 128
BF16_ROWS = 16
VMEM_LIMIT = 56 * 1024 * 1024

S5_SLABS = S5_WIDTH // LANES
SLAB_GROUPS = LANES // S5_CH
S5_CHUNK = 16
S5_BLOCK = S5_CHUNK * S5_CH
MAX_SCAN_STEPS = 10
S5_LANES = 1024

TM_PROJ = 512
TQ_ATTN = 1024
Q_BLOCK = 128
BLOCK_GROUP = 2
HALF_WIN = 64
TM_FFN = 512
TF_FFN = 256

_NT = (((1,), (1,)), ((), ()))


def _cparams(sem):
    return pltpu.CompilerParams(dimension_semantics=sem, vmem_limit_bytes=VMEM_LIMIT)


def _const_spec(shape):
    nd = len(shape)
    return pl.BlockSpec(shape, lambda *_: (0,) * nd)


def _proj_kernel(x_ref, g_ref, w_ref, gq_ref, gk_ref, u_ref, q_ref, k_ref, v_ref):
    x = x_ref[...]
    ms = jnp.mean(x * x, axis=-1, keepdims=True)
    n = (x * lax.rsqrt(ms + NORM_EPS) * g_ref[...]).astype(jnp.bfloat16)
    proj = jnp.dot(n, w_ref[...], preferred_element_type=jnp.float32)

    first_head = lax.broadcasted_iota(jnp.int32, (1, LANES), 1) < HEAD_DIM

    def head_norm(t, gain):
        sq = t * t
        tots = []
        for s in range(HEAD_PAIRS):
            blk = sq[:, s * LANES:(s + 1) * LANES]
            first = jnp.sum(jnp.where(first_head, blk, 0.0), axis=-1, keepdims=True)
            second = jnp.sum(jnp.where(first_head, 0.0, blk), axis=-1, keepdims=True)
            tots.append(jnp.where(first_head, first, second))
        tot = jnp.concatenate(tots, axis=1)
        return t * lax.rsqrt(tot * (1.0 / HEAD_DIM) + NORM_EPS) * gain

    u = proj[:, :S5_WIDTH]
    q = proj[:, S5_WIDTH:S5_WIDTH + ATTN_WIDTH]
    k = proj[:, S5_WIDTH + ATTN_WIDTH:S5_WIDTH + 2 * ATTN_WIDTH]
    q = head_norm(q, gq_ref[...]) * (HEAD_DIM ** -0.5)
    k = head_norm(k, gk_ref[...])
    v = proj[:, S5_WIDTH + 2 * ATTN_WIDTH:]
    for s in range(S5_SLABS):
        cols = slice(s * LANES, (s + 1) * LANES)
        u_ref[s] = u[:, cols]
        q_ref[s] = q[:, cols]
        k_ref[s] = k[:, cols]
        v_ref[s] = v[:, cols]


def _proj(x2, g, w_in, gq, gk):
    t = x2.shape[0]
    tm = TM_PROJ
    slab = pl.BlockSpec((HEAD_PAIRS, tm, LANES), lambda i: (0, i, 0))
    out = jax.ShapeDtypeStruct((HEAD_PAIRS, t, LANES), jnp.float32)
    return pl.pallas_call(
        _proj_kernel,
        grid=(t // tm,),
        in_specs=[pl.BlockSpec((tm, D_MODEL), lambda i: (i, 0)), _const_spec((1, D_MODEL)),
                  _const_spec((D_MODEL, D_IN)), _const_spec((1, ATTN_WIDTH)), _const_spec((1, ATTN_WIDTH))],
        out_specs=[slab, slab, slab, slab],
        out_shape=[out, out, out, out],
        compiler_params=_cparams(("parallel",)),
        name="proj",
    )(x2, g, w_in, gq, gk)


def _chunk_scan(xr, xi, tab_ref, gi, col0, pos, n_chunks, reverse):
    lanes = xr.shape[1]
    k, step = 1, 0
    while k < n_chunks:
        mr = tab_ref[gi, :, col0 + 2 * step:col0 + 2 * step + 1]
        mi = tab_ref[gi, :, col0 + 2 * step + 1:col0 + 2 * step + 2]
        keep = (pos < n_chunks - k) if reverse else (pos >= k)
        shift = (lanes - k) if reverse else k
        sr = jnp.where(keep, pltpu.roll(xr, shift, 1), 0.0)
        si = jnp.where(keep, pltpu.roll(xi, shift, 1), 0.0)
        xr, xi = xr + (mr * sr - mi * si), xi + (mr * si + mi * sr)
        k, step = 2 * k, step + 1
    return xr, xi


def _s5_kernel(u_ref, toep_ref, p_ref, q_ref, tab_ref, y_ref, xt_scr, yt_scr, *, n_chunks, n_seq):
    n = n_chunks
    lanes = n_seq * n
    for b in range(n_seq):
        for t in range(S5_CHUNK):
            rows = pl.ds(t, n, stride=S5_CHUNK)
            xt_scr[t, :, b * n:(b + 1) * n] = u_ref[0, b, rows, :].T.astype(jnp.bfloat16)
    pos = lax.broadcasted_iota(jnp.int32, (1, lanes), 1) % n
    ns = S5_STATE

    def per_group(gi, _):
        ch = pl.ds(pl.multiple_of(gi * S5_CH, S5_CH), S5_CH)
        x = jnp.concatenate([xt_scr[t, ch, :] for t in range(S5_CHUNK)], axis=0)
        y = jnp.dot(toep_ref[gi], x, preferred_element_type=jnp.float32)
        st = jnp.dot(p_ref[gi], x, preferred_element_type=jnp.float32)
        fr, fi = _chunk_scan(st[0:ns], st[ns:2 * ns], tab_ref, gi, 0, pos, n, False)
        br, bi = _chunk_scan(st[2 * ns:3 * ns], st[3 * ns:4 * ns], tab_ref, gi, 2 * MAX_SCAN_STEPS, pos, n, True)
        prev = lambda a: jnp.where(pos >= 1, pltpu.roll(a, 1, 1), 0.0)
        nxt = lambda a: jnp.where(pos < n - 1, pltpu.roll(a, lanes - 1, 1), 0.0)
        h = jnp.concatenate([prev(fr), prev(fi), nxt(br), nxt(bi)], axis=0).astype(jnp.bfloat16)
        y = y + jnp.dot(q_ref[gi], h, preferred_element_type=jnp.float32)
        for t in range(S5_CHUNK):
            yt_scr[t, ch, :] = y[t * S5_CH:(t + 1) * S5_CH, :]
        return 0

    lax.fori_loop(0, SLAB_GROUPS, per_group, 0)
    for b in range(n_seq):
        for t in range(S5_CHUNK):
            y_ref[0, b, pl.ds(t, n, stride=S5_CHUNK), :] = yt_scr[t, :, b * n:(b + 1) * n].T


def _s5(u, mats):
    _, b, s, _ = u.shape
    n = s // S5_CHUNK
    nb = max(1, min(b, S5_LANES // n))
    toep, pmat, qmat, tab = mats
    io = pl.BlockSpec((1, nb, s, LANES), lambda isl, ib: (isl, ib, 0, 0))
    wspec = pl.BlockSpec((SLAB_GROUPS, S5_BLOCK, S5_BLOCK), lambda isl, ib: (isl, 0, 0))
    return pl.pallas_call(
        functools.partial(_s5_kernel, n_chunks=n, n_seq=nb),
        grid=(S5_SLABS, b // nb),
        in_specs=[io, wspec, wspec, wspec,
                  pl.BlockSpec((SLAB_GROUPS, S5_STATE, LANES), lambda isl, ib: (isl, 0, 0))],
        out_specs=io,
        out_shape=jax.ShapeDtypeStruct(u.shape, jnp.float32),
        scratch_shapes=[pltpu.VMEM((S5_CHUNK, LANES, nb * n), jnp.bfloat16),
                        pltpu.VMEM((S5_CHUNK, LANES, nb * n), jnp.float32)],
        compiler_params=_cparams(("parallel", "parallel")),
        name="s5",
    )(u, toep, pmat, qmat, tab)


def _s5_matrices(a_re, a_im, log_dt, b_re, b_im, c_re, c_im, d_skip):
    f32 = jnp.float32
    hp = lax.Precision.HIGHEST
    n, g_, p_ = S5_CHUNK, S5_GROUPS, S5_STATE
    cmul = lambda a, b: (a[0] * b[0] - a[1] * b[1], a[0] * b[1] + a[1] * b[0])
    kern, p_rows, q_cols, tab_cols = [], [], [], []
    for direction in range(2):
        ar = a_re[direction].astype(f32)
        ai = a_im[direction].astype(f32)
        dt = jnp.exp(log_dt[direction].astype(f32))[:, None]
        mag = jnp.exp(ar * dt)
        lam = (mag * jnp.cos(ai * dt), mag * jnp.sin(ai * dt))
        den = ar * ar + ai * ai
        nr = lam[0] - 1.0
        fr = (nr * ar + lam[1] * ai) / den
        fi = (lam[1] * ar - nr * ai) / den
        br = b_re[direction].astype(f32)
        bi = b_im[direction].astype(f32)
        bbar = (fr[:, :, None] * br - fi[:, :, None] * bi, fr[:, :, None] * bi + fi[:, :, None] * br)
        cr = c_re[direction].astype(f32)
        ci = c_im[direction].astype(f32)
        powers = [(jnp.ones_like(lam[0]), jnp.zeros_like(lam[0]))]
        for _ in range(n):
            powers.append(cmul(powers[-1], lam))
        ks = list(range(n - 1, -1, -1)) if direction == 0 else list(range(n))
        pw = (jnp.stack([powers[k][0] for k in ks], axis=-1)[..., None],
              jnp.stack([powers[k][1] for k in ks], axis=-1)[..., None])
        m = cmul(pw, (bbar[0][:, :, None, :], bbar[1][:, :, None, :]))
        m = (m[0].reshape(g_, p_, S5_BLOCK), m[1].reshape(g_, p_, S5_BLOCK))
        p_rows += [m[0], m[1]]
        kern.append(jnp.einsum('gap,gpx->gax', cr, m[0], precision=hp)
                    - jnp.einsum('gap,gpx->gax', ci, m[1], precision=hp))
        ts = [t + 1 for t in range(n)] if direction == 0 else [n - t for t in range(n)]
        lt = (jnp.stack([powers[k][0] for k in ts], axis=1)[:, :, None, :],
              jnp.stack([powers[k][1] for k in ts], axis=1)[:, :, None, :])
        z = cmul((cr[:, None], ci[:, None]), lt)
        q_cols += [z[0].reshape(g_, S5_BLOCK, p_), -z[1].reshape(g_, S5_BLOCK, p_)]
        step = powers[n]
        for _ in range(MAX_SCAN_STEPS):
            tab_cols += [step[0], step[1]]
            step = cmul(step, step)
    last = (n - 1) * S5_CH
    eye = jnp.eye(S5_CH, dtype=f32)
    centre = kern[0][:, :, last:] + kern[1][:, :, :S5_CH] + d_skip.astype(f32).reshape(g_, S5_CH)[:, :, None] * eye
    by_lag = jnp.concatenate([kern[0][:, :, :last], centre, kern[1][:, :, S5_CH:]], axis=-1)
    toep = jnp.stack([by_lag[:, :, (n - 1 - t) * S5_CH:(n - 1 - t) * S5_CH + S5_BLOCK] for t in range(n)],
                     axis=1).reshape(g_, S5_BLOCK, S5_BLOCK)
    bf16 = jnp.bfloat16
    pmat = jnp.concatenate(p_rows, axis=1)
    qmat = jnp.concatenate(q_cols, axis=2)
    pad = jnp.zeros((g_, p_, LANES - len(tab_cols)), f32)
    tab = jnp.concatenate([jnp.stack(tab_cols, axis=-1), pad], axis=-1)
    return toep.astype(bf16), pmat.astype(bf16), qmat.astype(bf16), tab


def _attn_kernel(q_ref, kp_ref, km_ref, kn_ref, vp_ref, vm_ref, vn_ref, g_ref, o_ref,
                 o_scr, lse_scr, bias_scr, qd_scr, kd_scr, vd_scr, s_scr, p_scr, *, seq_len, tq):
    t0 = pl.program_id(1) * tq
    lane = lax.broadcasted_iota(jnp.int32, (1, LANES), 1)
    first_head = lane < HEAD_DIM

    @pl.when((pl.program_id(0) == 0) & (pl.program_id(1) == 0))
    def _():
        for ip, (_, dil) in enumerate(DILATED_PATTERNS):
            qb = min(Q_BLOCK, tq // dil)
            width = qb + 2 * HALF_WIN
            qi = lax.broadcasted_iota(jnp.int32, (qb, width), 0)
            kj = lax.broadcasted_iota(jnp.int32, (qb, width), 1) - HALF_WIN
            rel = jnp.abs(qi - kj)
            dist = (dil * rel).astype(jnp.float32)
            for h in range(N_HEADS):
                slope = 2.0 ** (-8.0 * (h + 1) / N_HEADS)
                bias_scr[ip, h // 2, (h % 2) * qb:(h % 2 + 1) * qb, 0:width] = jnp.where(
                    rel <= HALF_WIN, -slope * dist, NEG_INF)

    def blocks(descs):
        chains = [(bi, hp) for bi in range(len(descs)) for hp in range(HEAD_PAIRS)]

        def stage(c, qb):
            per_slot = Q_BLOCK // qb
            return c // per_slot, (c % per_slot) * 2 * qb

        for c, (bi, hp) in enumerate(chains):
            ip, dil, qb, q_of, k_of, _, _, first_key, check_ends = descs[bi]
            width = qb + 2 * HALF_WIN
            q2 = q_of(hp)
            k2 = k_of(hp)
            qq = jnp.concatenate([jnp.where(first_head, q2, 0.0), jnp.where(first_head, 0.0, q2)],
                                 axis=0).astype(jnp.bfloat16)
            s = lax.dot_general(qq, k2, _NT, preferred_element_type=jnp.float32)
            s = s + bias_scr[ip, hp, 0:2 * qb, 0:width]
            if check_ends:
                kpos = lax.broadcasted_iota(jnp.int32, (1, width), 1) + first_key
                s = jnp.where((kpos >= 0) & (kpos < seq_len // dil), s, NEG_INF)
            slot, r0 = stage(c, qb)
            s_scr[slot, r0:r0 + 2 * qb, 0:width] = s
        stats = []
        for c, (bi, hp) in enumerate(chains):
            qb = descs[bi][2]
            width = qb + 2 * HALF_WIN
            slot, r0 = stage(c, qb)
            s = s_scr[slot, r0:r0 + 2 * qb, 0:width]
            m = jnp.max(s, axis=-1, keepdims=True)
            p = jnp.exp(s - m)
            l = jnp.sum(p, axis=-1, keepdims=True)
            p_scr[slot, r0:r0 + 2 * qb, 0:width] = p.astype(jnp.bfloat16)
            stats.append((1.0 / l, m + jnp.log(l)))
        for c, (bi, hp) in enumerate(chains):
            ip, _, qb, _, _, v_of, out_rows, _, _ = descs[bi]
            width = qb + 2 * HALF_WIN
            inv_l, lse = stats[c]
            slot, r0 = stage(c, qb)
            o2 = jnp.dot(p_scr[slot, r0:r0 + 2 * qb, 0:width], v_of(hp), preferred_element_type=jnp.float32)
            o2 = o2 * inv_l
            o_scr[ip, hp, out_rows, :] = jnp.where(first_head, o2[:qb], o2[qb:])
            lse_scr[ip, hp, out_rows, :] = jnp.where(first_head, lse[:qb], lse[qb:])

    def dense_desc(i0, first, last):
        qb = Q_BLOCK
        own = pl.ds(i0, qb)

        def window(prev_ref, main_ref, next_ref, hp):
            lo = (prev_ref[hp, 0, pl.ds(tq - HALF_WIN, HALF_WIN), :] if first
                  else main_ref[hp, 0, pl.ds(i0 - HALF_WIN, HALF_WIN), :])
            hi = (next_ref[hp, 0, pl.ds(0, HALF_WIN), :] if last
                  else main_ref[hp, 0, pl.ds(i0 + qb, HALF_WIN), :])
            return jnp.concatenate([lo, main_ref[hp, 0, own, :], hi], axis=0).astype(jnp.bfloat16)

        return (0, 1, qb, lambda hp: q_ref[hp, 0, own, :],
                lambda hp: window(kp_ref, km_ref, kn_ref, hp), lambda hp: window(vp_ref, vm_ref, vn_ref, hp),
                own, t0 + i0 - HALF_WIN, first or last)

    nblk = tq // Q_BLOCK
    blocks([dense_desc(0, True, False), dense_desc((nblk - 1) * Q_BLOCK, False, True)])

    def dense_mid(j, _):
        i0 = pl.multiple_of((BLOCK_GROUP * j + 1) * Q_BLOCK, Q_BLOCK)
        blocks([dense_desc(i0 + g * Q_BLOCK, False, False) for g in range(BLOCK_GROUP)])
        return 0

    lax.fori_loop(0, (nblk - 2) // BLOCK_GROUP, dense_mid, 0)

    quarter = tq // 4

    def per_class(r4, _):
        for hp in range(HEAD_PAIRS):
            qd_scr[hp] = q_ref[hp, 0, pl.ds(r4, quarter, stride=4), :]
            for part, (kr, vr) in enumerate(((kp_ref, vp_ref), (km_ref, vm_ref), (kn_ref, vn_ref))):
                dst = pl.ds(part * quarter, quarter)
                kd_scr[hp, dst, :] = kr[hp, 0, pl.ds(r4, quarter, stride=4), :]
                vd_scr[hp, dst, :] = vr[hp, 0, pl.ds(r4, quarter, stride=4), :]

        def desc4(i0):
            win = pl.ds(quarter + i0 - HALF_WIN, Q_BLOCK + 2 * HALF_WIN)
            return (1, 4, Q_BLOCK, lambda hp: qd_scr[hp, pl.ds(i0, Q_BLOCK), :],
                    lambda hp: kd_scr[hp, win, :].astype(jnp.bfloat16),
                    lambda hp: vd_scr[hp, win, :].astype(jnp.bfloat16),
                    pl.ds(r4 + 4 * i0, Q_BLOCK, stride=4), t0 // 4 + i0 - HALF_WIN, True)

        for i0 in range(0, quarter, BLOCK_GROUP * Q_BLOCK):
            blocks([desc4(i0 + g * Q_BLOCK) for g in range(BLOCK_GROUP)])

        n16 = tq // 16

        def desc16(a):
            win = pl.ds(a + quarter - 4 * HALF_WIN, n16 + 2 * HALF_WIN, stride=4)
            return (2, 16, n16, lambda hp: qd_scr[hp, pl.ds(a, n16, stride=4), :],
                    lambda hp: kd_scr[hp, win, :].astype(jnp.bfloat16),
                    lambda hp: vd_scr[hp, win, :].astype(jnp.bfloat16),
                    pl.ds(4 * a + r4, n16, stride=16), t0 // 16 - HALF_WIN, True)

        blocks([desc16(a) for a in range(4)])
        return 0

    lax.fori_loop(0, 4, per_class, 0)

    outs = []
    for hp in range(HEAD_PAIRS):
        lses = [lse_scr[ip, hp] for ip in range(len(DILATED_PATTERNS))]
        top = functools.reduce(jnp.maximum, lses)
        ws = [jnp.exp(x - top) for x in lses]
        num = sum(w * o_scr[ip, hp] for ip, w in enumerate(ws))
        outs.append(num / sum(ws))
    o = jnp.concatenate(outs, axis=1)
    ms = jnp.mean(o * o, axis=-1, keepdims=True)
    o_ref[0] = (o * lax.rsqrt(ms + NORM_EPS) * g_ref[...]).astype(o_ref.dtype)


def _attn(q, k, v, g_out):
    _, b, s, _ = q.shape
    tq = TQ_ATTN
    nt = s // tq
    npat = len(DILATED_PATTERNS)
    blk = (HEAD_PAIRS, 1, tq, LANES)
    main = pl.BlockSpec(blk, lambda ib, it: (0, ib, it, 0))
    prev = pl.BlockSpec(blk, lambda ib, it: (0, ib, jnp.maximum(it - 1, 0), 0))
    nxt = pl.BlockSpec(blk, lambda ib, it: (0, ib, jnp.minimum(it + 1, nt - 1), 0))
    return pl.pallas_call(
        functools.partial(_attn_kernel, seq_len=s, tq=tq),
        grid=(b, nt),
        in_specs=[main, prev, main, nxt, prev, main, nxt, _const_spec((1, ATTN_WIDTH))],
        out_specs=pl.BlockSpec((1, tq, ATTN_WIDTH), lambda ib, it: (ib, it, 0)),
        out_shape=jax.ShapeDtypeStruct((b, s, ATTN_WIDTH), jnp.bfloat16),
        scratch_shapes=[pltpu.VMEM((npat, HEAD_PAIRS, tq, LANES), jnp.float32),
                        pltpu.VMEM((npat, HEAD_PAIRS, tq, LANES), jnp.float32),
                        pltpu.VMEM((npat, HEAD_PAIRS, 2 * Q_BLOCK, Q_BLOCK + 2 * HALF_WIN), jnp.float32),
                        pltpu.VMEM((HEAD_PAIRS, tq // 4, LANES), jnp.float32),
                        pltpu.VMEM((HEAD_PAIRS, 3 * tq // 4, LANES), jnp.float32),
                        pltpu.VMEM((HEAD_PAIRS, 3 * tq // 4, LANES), jnp.float32),
                        pltpu.VMEM((BLOCK_GROUP * HEAD_PAIRS, 2 * Q_BLOCK, Q_BLOCK + 2 * HALF_WIN), jnp.float32),
                        pltpu.VMEM((BLOCK_GROUP * HEAD_PAIRS, 2 * Q_BLOCK, Q_BLOCK + 2 * HALF_WIN), jnp.bfloat16)],
        compiler_params=_cparams(("arbitrary", "arbitrary")),
        name="attn",
    )(q, k, k, k, v, v, v, g_out)


def _mix_ffn_kernel(xm_ref, xp_ref, xn_ref, ym_ref, yp_ref, yn_ref, bm_ref, bp_ref, bn_ref,
                    wglu_ref, bglu_ref, ga_ref, wa_ref, wb_ref, gf_ref, wup_ref, cw_ref, cb_ref, wdn_ref,
                    o_ref, h_scr, act_scr, *, tiles_per_seq, tm):
    pos = pl.program_id(0) % tiles_per_seq
    halo = BF16_ROWS
    rows = tm + 2 * halo
    x = jnp.concatenate([xp_ref[...], xm_ref[...], xn_ref[...]], axis=0)
    y = jnp.concatenate(
        [jnp.concatenate([yp_ref[s], ym_ref[s], yn_ref[s]], axis=0) for s in range(S5_SLABS)], axis=1)
    b_n = jnp.concatenate([bp_ref[...], bm_ref[...], bn_ref[...]], axis=0)
    z = jax.nn.gelu(y)
    gate = jax.nn.sigmoid(
        jnp.dot(z.astype(jnp.bfloat16), wglu_ref[...], preferred_element_type=jnp.float32) + bglu_ref[...])
    a = z * gate
    ms = jnp.mean(a * a, axis=-1, keepdims=True)
    a_n = (a * lax.rsqrt(ms + NORM_EPS) * ga_ref[...]).astype(jnp.bfloat16)
    x1 = (x + jnp.dot(a_n, wa_ref[...], preferred_element_type=jnp.float32)
          + jnp.dot(b_n, wb_ref[...], preferred_element_type=jnp.float32))
    ms = jnp.mean(x1 * x1, axis=-1, keepdims=True)
    n = x1 * lax.rsqrt(ms + NORM_EPS) * gf_ref[...]
    r = lax.broadcasted_iota(jnp.int32, (rows, 1), 0)
    outside = ((r < halo) & (pos == 0)) | ((r >= halo + tm) & (pos == tiles_per_seq - 1))
    nh = jnp.where(outside, 0.0, n).astype(jnp.bfloat16)
    slabs = TF_FFN // LANES

    def conv(h, off, base):
        outs = []
        for s in range(slabs):
            h_scr[base + s] = h[:, s * LANES:(s + 1) * LANES]
            cols = pl.ds(off + s * LANES, LANES)
            outs.append(h_scr[base + s, pl.ds(halo - 1, tm), :] * cw_ref[0:1, cols]
                        + h_scr[base + s, pl.ds(halo, tm), :] * cw_ref[1:2, cols]
                        + h_scr[base + s, pl.ds(halo + 1, tm), :] * cw_ref[2:3, cols]
                        + cb_ref[:, cols])
        return jnp.concatenate(outs, axis=1)

    for j in range(D_FF // TF_FFN):
        off = j * TF_FFN
        base = (j % 2) * 2 * slabs
        hg = jnp.dot(nh, wup_ref[:, pl.ds(off, TF_FFN)], preferred_element_type=jnp.float32)
        hu = jnp.dot(nh, wup_ref[:, pl.ds(D_FF + off, TF_FFN)], preferred_element_type=jnp.float32)
        g = conv(hg, off, base)
        up = conv(hu, D_FF + off, base + slabs)
        act_scr[:, pl.ds(off, TF_FFN)] = (g * jax.nn.sigmoid(g) * up).astype(jnp.bfloat16)
    o_ref[...] = x1[halo:halo + tm] + jnp.dot(act_scr[...], wdn_ref[...], preferred_element_type=jnp.float32)


def _mix_ffn(x2, y, b_n, p, seq_len):
    t = x2.shape[0]
    tm = TM_FFN
    halo = BF16_ROWS
    hb = tm // halo
    nhb = t // halo
    prev_i = lambda i: jnp.maximum(i * hb - 1, 0)
    next_i = lambda i: jnp.minimum((i + 1) * hb, nhb - 1)
    row = lambda w: pl.BlockSpec((tm, w), lambda i: (i, 0))
    rowp = lambda w: pl.BlockSpec((halo, w), lambda i: (prev_i(i), 0))
    rown = lambda w: pl.BlockSpec((halo, w), lambda i: (next_i(i), 0))
    slab = lambda r, f: pl.BlockSpec((S5_SLABS, r, LANES), lambda i: (0, f(i), 0))
    once = lambda shape: pl.BlockSpec(shape, lambda i: (0,) * len(shape), pipeline_mode=pl.Buffered(1))
    return pl.pallas_call(
        functools.partial(_mix_ffn_kernel, tiles_per_seq=seq_len // tm, tm=tm),
        grid=(t // tm,),
        in_specs=[row(D_MODEL), rowp(D_MODEL), rown(D_MODEL),
                  slab(tm, lambda i: i), slab(halo, prev_i), slab(halo, next_i),
                  row(ATTN_WIDTH), rowp(ATTN_WIDTH), rown(ATTN_WIDTH),
                  once((S5_WIDTH, S5_WIDTH)), once((1, S5_WIDTH)), once((1, S5_WIDTH)),
                  once((S5_WIDTH, D_MODEL)), once((ATTN_WIDTH, D_MODEL)), once((1, D_MODEL)),
                  once((D_MODEL, 2 * D_FF)), once((3, 2 * D_FF)), once((1, 2 * D_FF)), once((D_FF, D_MODEL))],
        out_specs=row(D_MODEL),
        out_shape=jax.ShapeDtypeStruct((t, D_MODEL), jnp.float32),
        scratch_shapes=[pltpu.VMEM((4 * TF_FFN // LANES, tm + 2 * halo, LANES), jnp.float32),
                        pltpu.VMEM((tm, D_FF), jnp.bfloat16)],
        compiler_params=_cparams(("parallel",)),
        name="mix_ffn",
    )(x2, x2, x2, y, y, y, b_n, b_n, b_n, p["w_glu"], p["b_glu"], p["ssm_out_g"], p["w_out_a"], p["w_out_b"],
      p["norm_ffn_g"], p["w_up"], p["conv_w"], p["conv_b"], p["w_down"])


def _layer(x, p):
    b, s, d = x.shape
    x2 = x.reshape(b * s, d)
    u, q, k, v = _proj(x2, p["norm_mix_g"], p["w_in"], p["gq"], p["gk"])
    shp = (HEAD_PAIRS, b, s, LANES)
    y = _s5(u.reshape(shp), p["s5_mats"])
    b_n = _attn(q.reshape(shp), k.reshape(shp), v.reshape(shp), p["attn_out_g"])
    out = _mix_ffn(x2, y.reshape(S5_SLABS, b * s, LANES), b_n.reshape(b * s, ATTN_WIDTH), p, s)
    return out.reshape(b, s, d)


def kernel(x_prompt, x_sample, norm_mix_g, w_in, s5_a_re, s5_a_im, s5_log_dt, s5_b_re, s5_b_im, s5_c_re, s5_c_im, s5_d, w_glu, b_glu, q_norm_g, k_norm_g, ssm_out_g, attn_out_g, w_out, norm_ffn_g, w_up, conv_w, conv_b, w_down):
    depth = w_in.shape[0]
    f32, bf16 = jnp.float32, jnp.bfloat16
    y_prompt, y_sample = x_prompt, x_sample
    for i in range(depth):
        p = {
            "norm_mix_g": norm_mix_g[i].astype(f32)[None],
            "w_in": w_in[i].astype(bf16),
            "gq": jnp.tile(q_norm_g[i].astype(f32), N_HEADS)[None],
            "gk": jnp.tile(k_norm_g[i].astype(f32), N_HEADS)[None],
            "s5_mats": _s5_matrices(s5_a_re[i], s5_a_im[i], s5_log_dt[i], s5_b_re[i], s5_b_im[i],
                                    s5_c_re[i], s5_c_im[i], s5_d[i]),
            "w_glu": w_glu[i].astype(bf16),
            "b_glu": b_glu[i].astype(f32)[None],
            "ssm_out_g": ssm_out_g[i].astype(f32)[None],
            "attn_out_g": attn_out_g[i].astype(f32)[None],
            "w_out_a": w_out[i, :S5_WIDTH].astype(bf16),
            "w_out_b": w_out[i, S5_WIDTH:].astype(bf16),
            "norm_ffn_g": norm_ffn_g[i].astype(f32)[None],
            "w_up": w_up[i].astype(bf16),
            "conv_w": conv_w[i].astype(f32),
            "conv_b": conv_b[i].astype(f32)[None],
            "w_down": w_down[i].astype(bf16),
        }
        y_prompt = _layer(y_prompt, p)
        y_sample = _layer(y_sample, p)
    return (y_prompt, y_sample)
```

```python
import functools

import jax
import jax.numpy as jnp
from jax import lax
from jax.experimental import pallas as pl
from jax.experimental.pallas import tpu as pltpu

D_MODEL = 1024
S5_WIDTH = 512
S5_CH = 16
S5_GROUPS = 32
S5_STATE = 64
ATTN_WIDTH = 512
HEAD_DIM = 64
N_HEADS = 8
HEAD_PAIRS = N_HEADS // 2
DILATED_PATTERNS = ((128, 1), (512, 4), (2048, 16))
D_IN = S5_WIDTH + 3 * ATTN_WIDTH
D_FF = 2816
NORM_EPS = 1e-6
NEG_INF = -1e30

LANES = 128
BF16_ROWS = 16
VMEM_LIMIT = 56 * 1024 * 1024

S5_SLABS = S5_WIDTH // LANES
SLAB_GROUPS = LANES // S5_CH
S5_CHUNK = 16
S5_BLOCK = S5_CHUNK * S5_CH
MAX_SCAN_STEPS = 10
S5_LANES = 1024

TM_PROJ = 512
TQ_ATTN = 1024
Q_BLOCK = 128
BLOCK_GROUP = 2
HALF_WIN = 64
TM_FFN = 512
TF_FFN = 256

_NT = (((1,), (1,)), ((), ()))


def _cparams(sem):
    return pltpu.CompilerParams(dimension_semantics=sem, vmem_limit_bytes=VMEM_LIMIT)


def _const_spec(shape):
    nd = len(shape)
    return pl.BlockSpec(shape, lambda *_: (0,) * nd)


def _proj_kernel(x_ref, g_ref, w_ref, gq_ref, gk_ref, u_ref, q_ref, k_ref, v_ref):
    x = x_ref[...]
    ms = jnp.mean(x * x, axis=-1, keepdims=True)
    n = (x * lax.rsqrt(ms + NORM_EPS) * g_ref[...]).astype(jnp.bfloat16)
    proj = jnp.dot(n, w_ref[...], preferred_element_type=jnp.float32)

    first_head = lax.broadcasted_iota(jnp.int32, (1, LANES), 1) < HEAD_DIM

    def head_norm(t, gain):
        sq = t * t
        tots = []
        for s in range(HEAD_PAIRS):
            blk = sq[:, s * LANES:(s + 1) * LANES]
            first = jnp.sum(jnp.where(first_head, blk, 0.0), axis=-1, keepdims=True)
            second = jnp.sum(jnp.where(first_head, 0.0, blk), axis=-1, keepdims=True)
            tots.append(jnp.where(first_head, first, second))
        tot = jnp.concatenate(tots, axis=1)
        return t * lax.rsqrt(tot * (1.0 / HEAD_DIM) + NORM_EPS) * gain

    u = proj[:, :S5_WIDTH]
    q = proj[:, S5_WIDTH:S5_WIDTH + ATTN_WIDTH]
    k = proj[:, S5_WIDTH + ATTN_WIDTH:S5_WIDTH + 2 * ATTN_WIDTH]
    q = head_norm(q, gq_ref[...]) * (HEAD_DIM ** -0.5)
    k = head_norm(k, gk_ref[...])
    v = proj[:, S5_WIDTH + 2 * ATTN_WIDTH:]
    for s in range(S5_SLABS):
        cols = slice(s * LANES, (s + 1) * LANES)
        u_ref[s] = u[:, cols]
        q_ref[s] = q[:, cols]
        k_ref[s] = k[:, cols]
        v_ref[s] = v[:, cols]


def _proj(x2, g, w_in, gq, gk):
    t = x2.shape[0]
    tm = TM_PROJ
    slab = pl.BlockSpec((HEAD_PAIRS, tm, LANES), lambda i: (0, i, 0))
    out = jax.ShapeDtypeStruct((HEAD_PAIRS, t, LANES), jnp.float32)
    return pl.pallas_call(
        _proj_kernel,
        grid=(t // tm,),
        in_specs=[pl.BlockSpec((tm, D_MODEL), lambda i: (i, 0)), _const_spec((1, D_MODEL)),
                  _const_spec((D_MODEL, D_IN)), _const_spec((1, ATTN_WIDTH)), _const_spec((1, ATTN_WIDTH))],
        out_specs=[slab, slab, slab, slab],
        out_shape=[out, out, out, out],
        compiler_params=_cparams(("parallel",)),
        name="proj",
    )(x2, g, w_in, gq, gk)


def _chunk_scan(xr, xi, tab_ref, gi, col0, pos, n_chunks, reverse):
    lanes = xr.shape[1]
    k, step = 1, 0
    while k < n_chunks:
        mr = tab_ref[gi, :, col0 + 2 * step:col0 + 2 * step + 1]
        mi = tab_ref[gi, :, col0 + 2 * step + 1:col0 + 2 * step + 2]
        keep = (pos < n_chunks - k) if reverse else (pos >= k)
        shift = (lanes - k) if reverse else k
        sr = jnp.where(keep, pltpu.roll(xr, shift, 1), 0.0)
        si = jnp.where(keep, pltpu.roll(xi, shift, 1), 0.0)
        xr, xi = xr + (mr * sr - mi * si), xi + (mr * si + mi * sr)
        k, step = 2 * k, step + 1
    return xr, xi


def _s5_kernel(u_ref, toep_ref, p_ref, q_ref, tab_ref, y_ref, xt_scr, yt_scr, *, n_chunks, n_seq):
    n = n_chunks
    lanes = n_seq * n
    for b in range(n_seq):
        for t in range(S5_CHUNK):
            rows = pl.ds(t, n, stride=S5_CHUNK)
            xt_scr[t, :, b * n:(b + 1) * n] = u_ref[0, b, rows, :].T.astype(jnp.bfloat16)
    pos = lax.broadcasted_iota(jnp.int32, (1, lanes), 1) % n
    ns = S5_STATE

    def per_group(gi, _):
        ch = pl.ds(pl.multiple_of(gi * S5_CH, S5_CH), S5_CH)
        x = jnp.concatenate([xt_scr[t, ch, :] for t in range(S5_CHUNK)], axis=0)
        y = jnp.dot(toep_ref[gi], x, preferred_element_type=jnp.float32)
        st = jnp.dot(p_ref[gi], x, preferred_element_type=jnp.float32)
        fr, fi = _chunk_scan(st[0:ns], st[ns:2 * ns], tab_ref, gi, 0, pos, n, False)
        br, bi = _chunk_scan(st[2 * ns:3 * ns], st[3 * ns:4 * ns], tab_ref, gi, 2 * MAX_SCAN_STEPS, pos, n, True)
        prev = lambda a: jnp.where(pos >= 1, pltpu.roll(a, 1, 1), 0.0)
        nxt = lambda a: jnp.where(pos < n - 1, pltpu.roll(a, lanes - 1, 1), 0.0)
        h = jnp.concatenate([prev(fr), prev(fi), nxt(br), nxt(bi)], axis=0).astype(jnp.bfloat16)
        y = y + jnp.dot(q_ref[gi], h, preferred_element_type=jnp.float32)
        for t in range(S5_CHUNK):
            yt_scr[t, ch, :] = y[t * S5_CH:(t + 1) * S5_CH, :]
        return 0

    lax.fori_loop(0, SLAB_GROUPS, per_group, 0)
    for b in range(n_seq):
        for t in range(S5_CHUNK):
            y_ref[0, b, pl.ds(t, n, stride=S5_CHUNK), :] = yt_scr[t, :, b * n:(b + 1) * n].T


def _s5(u, mats):
    _, b, s, _ = u.shape
    n = s // S5_CHUNK
    nb = max(1, min(b, S5_LANES // n))
    toep, pmat, qmat, tab = mats
    io = pl.BlockSpec((1, nb, s, LANES), lambda isl, ib: (isl, ib, 0, 0))
    wspec = pl.BlockSpec((SLAB_GROUPS, S5_BLOCK, S5_BLOCK), lambda isl, ib: (isl, 0, 0))
    return pl.pallas_call(
        functools.partial(_s5_kernel, n_chunks=n, n_seq=nb),
        grid=(S5_SLABS, b // nb),
        in_specs=[io, wspec, wspec, wspec,
                  pl.BlockSpec((SLAB_GROUPS, S5_STATE, LANES), lambda isl, ib: (isl, 0, 0))],
        out_specs=io,
        out_shape=jax.ShapeDtypeStruct(u.shape, jnp.float32),
        scratch_shapes=[pltpu.VMEM((S5_CHUNK, LANES, nb * n), jnp.bfloat16),
                        pltpu.VMEM((S5_CHUNK, LANES, nb * n), jnp.float32)],
        compiler_params=_cparams(("parallel", "parallel")),
        name="s5",
    )(u, toep, pmat, qmat, tab)


def _s5_matrices(a_re, a_im, log_dt, b_re, b_im, c_re, c_im, d_skip):
    f32 = jnp.float32
    hp = lax.Precision.HIGHEST
    n, g_, p_ = S5_CHUNK, S5_GROUPS, S5_STATE
    cmul = lambda a, b: (a[0] * b[0] - a[1] * b[1], a[0] * b[1] + a[1] * b[0])
    kern, p_rows, q_cols, tab_cols = [], [], [], []
    for direction in range(2):
        ar = a_re[direction].astype(f32)
        ai = a_im[direction].astype(f32)
        dt = jnp.exp(log_dt[direction].astype(f32))[:, None]
        mag = jnp.exp(ar * dt)
        lam = (mag * jnp.cos(ai * dt), mag * jnp.sin(ai * dt))
        den = ar * ar + ai * ai
        nr = lam[0] - 1.0
        fr = (nr * ar + lam[1] * ai) / den
        fi = (lam[1] * ar - nr * ai) / den
        br = b_re[direction].astype(f32)
        bi = b_im[direction].astype(f32)
        bbar = (fr[:, :, None] * br - fi[:, :, None] * bi, fr[:, :, None] * bi + fi[:, :, None] * br)
        cr = c_re[direction].astype(f32)
        ci = c_im[direction].astype(f32)
        powers = [(jnp.ones_like(lam[0]), jnp.zeros_like(lam[0]))]
        for _ in range(n):
            powers.append(cmul(powers[-1], lam))
        ks = list(range(n - 1, -1, -1)) if direction == 0 else list(range(n))
        pw = (jnp.stack([powers[k][0] for k in ks], axis=-1)[..., None],
              jnp.stack([powers[k][1] for k in ks], axis=-1)[..., None])
        m = cmul(pw, (bbar[0][:, :, None, :], bbar[1][:, :, None, :]))
        m = (m[0].reshape(g_, p_, S5_BLOCK), m[1].reshape(g_, p_, S5_BLOCK))
        p_rows += [m[0], m[1]]
        kern.append(jnp.einsum('gap,gpx->gax', cr, m[0], precision=hp)
                    - jnp.einsum('gap,gpx->gax', ci, m[1], precision=hp))
        ts = [t + 1 for t in range(n)] if direction == 0 else [n - t for t in range(n)]
        lt = (jnp.stack([powers[k][0] for k in ts], axis=1)[:, :, None, :],
              jnp.stack([powers[k][1] for k in ts], axis=1)[:, :, None, :])
        z = cmul((cr[:, None], ci[:, None]), lt)
        q_cols += [z[0].reshape(g_, S5_BLOCK, p_), -z[1].reshape(g_, S5_BLOCK, p_)]
        step = powers[n]
        for _ in range(MAX_SCAN_STEPS):
            tab_cols += [step[0], step[1]]
            step = cmul(step, step)
    last = (n - 1) * S5_CH
    eye = jnp.eye(S5_CH, dtype=f32)
    centre = kern[0][:, :, last:] + kern[1][:, :, :S5_CH] + d_skip.astype(f32).reshape(g_, S5_CH)[:, :, None] * eye
    by_lag = jnp.concatenate([kern[0][:, :, :last], centre, kern[1][:, :, S5_CH:]], axis=-1)
    toep = jnp.stack([by_lag[:, :, (n - 1 - t) * S5_CH:(n - 1 - t) * S5_CH + S5_BLOCK] for t in range(n)],
                     axis=1).reshape(g_, S5_BLOCK, S5_BLOCK)
    bf16 = jnp.bfloat16
    pmat = jnp.concatenate(p_rows, axis=1)
    qmat = jnp.concatenate(q_cols, axis=2)
    pad = jnp.zeros((g_, p_, LANES - len(tab_cols)), f32)
    tab = jnp.concatenate([jnp.stack(tab_cols, axis=-1), pad], axis=-1)
    return toep.astype(bf16), pmat.astype(bf16), qmat.astype(bf16), tab


def _attn_kernel(q_ref, kp_ref, km_ref, kn_ref, vp_ref, vm_ref, vn_ref, g_ref, o_ref,
                 o_scr, lse_scr, bias_scr, qd_scr, kd_scr, vd_scr, s_scr, p_scr, *, seq_len, tq):
    t0 = pl.program_id(1) * tq
    lane = lax.broadcasted_iota(jnp.int32, (1, LANES), 1)
    first_head = lane < HEAD_DIM

    @pl.when((pl.program_id(0) == 0) & (pl.program_id(1) == 0))
    def _():
        for ip, (_, dil) in enumerate(DILATED_PATTERNS):
            qb = min(Q_BLOCK, tq // dil)
            width = qb + 2 * HALF_WIN
            qi = lax.broadcasted_iota(jnp.int32, (qb, width), 0)
            kj = lax.broadcasted_iota(jnp.int32, (qb, width), 1) - HALF_WIN
            rel = jnp.abs(qi - kj)
            dist = (dil * rel).astype(jnp.float32)
            for h in range(N_HEADS):
                slope = 2.0 ** (-8.0 * (h + 1) / N_HEADS)
                bias_scr[ip, h // 2, (h % 2) * qb:(h % 2 + 1) * qb, 0:width] = jnp.where(
                    rel <= HALF_WIN, -slope * dist, NEG_INF)

    def blocks(descs):
        chains = [(bi, hp) for bi in range(len(descs)) for hp in range(HEAD_PAIRS)]

        def stage(c, qb):
            per_slot = Q_BLOCK // qb
            return c // per_slot, (c % per_slot) * 2 * qb

        for c, (bi, hp) in enumerate(chains):
            ip, dil, qb, q_of, k_of, _, _, first_key, check_ends = descs[bi]
            width = qb + 2 * HALF_WIN
            q2 = q_of(hp)
            k2 = k_of(hp)
            qq = jnp.concatenate([jnp.where(first_head, q2, 0.0), jnp.where(first_head, 0.0, q2)],
                                 axis=0).astype(jnp.bfloat16)
            s = lax.dot_general(qq, k2, _NT, preferred_element_type=jnp.float32)
            s = s + bias_scr[ip, hp, 0:2 * qb, 0:width]
            if check_ends:
                kpos = lax.broadcasted_iota(jnp.int32, (1, width), 1) + first_key
                s = jnp.where((kpos >= 0) & (kpos < seq_len // dil), s, NEG_INF)
            slot, r0 = stage(c, qb)
            s_scr[slot, r0:r0 + 2 * qb, 0:width] = s
        stats = []
        for c, (bi, hp) in enumerate(chains):
            qb = descs[bi][2]
            width = qb + 2 * HALF_WIN
            slot, r0 = stage(c, qb)
            s = s_scr[slot, r0:r0 + 2 * qb, 0:width]
            m = jnp.max(s, axis=-1, keepdims=True)
            p = jnp.exp((s - m).astype(jnp.bfloat16))
            l = jnp.sum(p.astype(jnp.float32), axis=-1, keepdims=True)
            p_scr[slot, r0:r0 + 2 * qb, 0:width] = p
            stats.append((1.0 / l, m + jnp.log(l)))
        for c, (bi, hp) in enumerate(chains):
            ip, _, qb, _, _, v_of, out_rows, _, _ = descs[bi]
            width = qb + 2 * HALF_WIN
            inv_l, lse = stats[c]
            slot, r0 = stage(c, qb)
            o2 = jnp.dot(p_scr[slot, r0:r0 + 2 * qb, 0:width], v_of(hp), preferred_element_type=jnp.float32)
            o2 = o2 * inv_l
            o_scr[ip, hp, out_rows, :] = jnp.where(first_head, o2[:qb], o2[qb:])
            lse_scr[ip, hp, out_rows, :] = jnp.where(first_head, lse[:qb], lse[qb:])

    def dense_desc(i0, first, last):
        qb = Q_BLOCK
        own = pl.ds(i0, qb)

        def window(prev_ref, main_ref, next_ref, hp):
            lo = (prev_ref[hp, 0, pl.ds(tq - HALF_WIN, HALF_WIN), :] if first
                  else main_ref[hp, 0, pl.ds(i0 - HALF_WIN, HALF_WIN), :])
            hi = (next_ref[hp, 0, pl.ds(0, HALF_WIN), :] if last
                  else main_ref[hp, 0, pl.ds(i0 + qb, HALF_WIN), :])
            return jnp.concatenate([lo, main_ref[hp, 0, own, :], hi], axis=0).astype(jnp.bfloat16)

        return (0, 1, qb, lambda hp: q_ref[hp, 0, own, :],
                lambda hp: window(kp_ref, km_ref, kn_ref, hp), lambda hp: window(vp_ref, vm_ref, vn_ref, hp),
                own, t0 + i0 - HALF_WIN, first or last)

    nblk = tq // Q_BLOCK
    blocks([dense_desc(0, True, False), dense_desc((nblk - 1) * Q_BLOCK, False, True)])

    def dense_mid(j, _):
        i0 = pl.multiple_of((BLOCK_GROUP * j + 1) * Q_BLOCK, Q_BLOCK)
        blocks([dense_desc(i0 + g * Q_BLOCK, False, False) for g in range(BLOCK_GROUP)])
        return 0

    lax.fori_loop(0, (nblk - 2) // BLOCK_GROUP, dense_mid, 0)

    quarter = tq // 4

    def per_class(r4, _):
        for hp in range(HEAD_PAIRS):
            qd_scr[hp] = q_ref[hp, 0, pl.ds(r4, quarter, stride=4), :]
            for part, (kr, vr) in enumerate(((kp_ref, vp_ref), (km_ref, vm_ref), (kn_ref, vn_ref))):
                dst = pl.ds(part * quarter, quarter)
                kd_scr[hp, dst, :] = kr[hp, 0, pl.ds(r4, quarter, stride=4), :]
                vd_scr[hp, dst, :] = vr[hp, 0, pl.ds(r4, quarter, stride=4), :]

        def desc4(i0):
            win = pl.ds(quarter + i0 - HALF_WIN, Q_BLOCK + 2 * HALF_WIN)
            return (1, 4, Q_BLOCK, lambda hp: qd_scr[hp, pl.ds(i0, Q_BLOCK), :],
                    lambda hp: kd_scr[hp, win, :].astype(jnp.bfloat16),
                    lambda hp: vd_scr[hp, win, :].astype(jnp.bfloat16),
                    pl.ds(r4 + 4 * i0, Q_BLOCK, stride=4), t0 // 4 + i0 - HALF_WIN, True)

        for i0 in range(0, quarter, BLOCK_GROUP * Q_BLOCK):
            blocks([desc4(i0 + g * Q_BLOCK) for g in range(BLOCK_GROUP)])

        n16 = tq // 16

        def desc16(a):
            win = pl.ds(a + quarter - 4 * HALF_WIN, n16 + 2 * HALF_WIN, stride=4)
            return (2, 16, n16, lambda hp: qd_scr[hp, pl.ds(a, n16, stride=4), :],
                    lambda hp: kd_scr[hp, win, :].astype(jnp.bfloat16),
                    lambda hp: vd_scr[hp, win, :].astype(jnp.bfloat16),
                    pl.ds(4 * a + r4, n16, stride=16), t0 // 16 - HALF_WIN, True)

        blocks([desc16(a) for a in range(4)])
        return 0

    lax.fori_loop(0, 4, per_class, 0)

    outs = []
    for hp in range(HEAD_PAIRS):
        lses = [lse_scr[ip, hp] for ip in range(len(DILATED_PATTERNS))]
        top = functools.reduce(jnp.maximum, lses)
        ws = [jnp.exp(x - top) for x in lses]
        num = sum(w * o_scr[ip, hp] for ip, w in enumerate(ws))
        outs.append(num / sum(ws))
    o = jnp.concatenate(outs, axis=1)
    ms = jnp.mean(o * o, axis=-1, keepdims=True)
    o_ref[0] = (o * lax.rsqrt(ms + NORM_EPS) * g_ref[...]).astype(o_ref.dtype)


def _attn(q, k, v, g_out):
    _, b, s, _ = q.shape
    tq = TQ_ATTN
    nt = s // tq
    npat = len(DILATED_PATTERNS)
    blk = (HEAD_PAIRS, 1, tq, LANES)
    main = pl.BlockSpec(blk, lambda ib, it: (0, ib, it, 0))
    prev = pl.BlockSpec(blk, lambda ib, it: (0, ib, jnp.maximum(it - 1, 0), 0))
    nxt = pl.BlockSpec(blk, lambda ib, it: (0, ib, jnp.minimum(it + 1, nt - 1), 0))
    return pl.pallas_call(
        functools.partial(_attn_kernel, seq_len=s, tq=tq),
        grid=(b, nt),
        in_specs=[main, prev, main, nxt, prev, main, nxt, _const_spec((1, ATTN_WIDTH))],
        out_specs=pl.BlockSpec((1, tq, ATTN_WIDTH), lambda ib, it: (ib, it, 0)),
        out_shape=jax.ShapeDtypeStruct((b, s, ATTN_WIDTH), jnp.bfloat16),
        scratch_shapes=[pltpu.VMEM((npat, HEAD_PAIRS, tq, LANES), jnp.float32),
                        pltpu.VMEM((npat, HEAD_PAIRS, tq, LANES), jnp.float32),
                        pltpu.VMEM((npat, HEAD_PAIRS, 2 * Q_BLOCK, Q_BLOCK + 2 * HALF_WIN), jnp.float32),
                        pltpu.VMEM((HEAD_PAIRS, tq // 4, LANES), jnp.float32),
                        pltpu.VMEM((HEAD_PAIRS, 3 * tq // 4, LANES), jnp.float32),
                        pltpu.VMEM((HEAD_PAIRS, 3 * tq // 4, LANES), jnp.float32),
                        pltpu.VMEM((BLOCK_GROUP * HEAD_PAIRS, 2 * Q_BLOCK, Q_BLOCK + 2 * HALF_WIN), jnp.float32),
                        pltpu.VMEM((BLOCK_GROUP * HEAD_PAIRS, 2 * Q_BLOCK, Q_BLOCK + 2 * HALF_WIN), jnp.bfloat16)],
        compiler_params=_cparams(("arbitrary", "arbitrary")),
        name="attn",
    )(q, k, k, k, v, v, v, g_out)


def _mix_ffn_kernel(xm_ref, xp_ref, xn_ref, ym_ref, yp_ref, yn_ref, bm_ref, bp_ref, bn_ref,
                    wglu_ref, bglu_ref, ga_ref, wa_ref, wb_ref, gf_ref, wup_ref, cw_ref, cb_ref, wdn_ref,
                    o_ref, h_scr, act_scr, *, tiles_per_seq, tm):
    pos = pl.program_id(0) % tiles_per_seq
    halo = BF16_ROWS
    rows = tm + 2 * halo
    x = jnp.concatenate([xp_ref[...], xm_ref[...], xn_ref[...]], axis=0)
    y = jnp.concatenate(
        [jnp.concatenate([yp_ref[s], ym_ref[s], yn_ref[s]], axis=0) for s in range(S5_SLABS)], axis=1)
    b_n = jnp.concatenate([bp_ref[...], bm_ref[...], bn_ref[...]], axis=0)
    z = jax.nn.gelu(y)
    gate = jax.nn.sigmoid(
        jnp.dot(z.astype(jnp.bfloat16), wglu_ref[...], preferred_element_type=jnp.float32) + bglu_ref[...])
    a = z * gate
    ms = jnp.mean(a * a, axis=-1, keepdims=True)
    a_n = (a * lax.rsqrt(ms + NORM_EPS) * ga_ref[...]).astype(jnp.bfloat16)
    x1 = (x + jnp.dot(a_n, wa_ref[...], preferred_element_type=jnp.float32)
          + jnp.dot(b_n, wb_ref[...], preferred_element_type=jnp.float32))
    ms = jnp.mean(x1 * x1, axis=-1, keepdims=True)
    n = x1 * lax.rsqrt(ms + NORM_EPS) * gf_ref[...]
    r = lax.broadcasted_iota(jnp.int32, (rows, 1), 0)
    outside = ((r < halo) & (pos == 0)) | ((r >= halo + tm) & (pos == tiles_per_seq - 1))
    nh = jnp.where(outside, 0.0, n).astype(jnp.bfloat16)
    slabs = TF_FFN // LANES

    def conv(h, off, base):
        outs = []
        for s in range(slabs):
            h_scr[base + s] = h[:, s * LANES:(s + 1) * LANES]
            cols = pl.ds(off + s * LANES, LANES)
            outs.append(h_scr[base + s, pl.ds(halo - 1, tm), :] * cw_ref[0:1, cols]
                        + h_scr[base + s, pl.ds(halo, tm), :] * cw_ref[1:2, cols]
                        + h_scr[base + s, pl.ds(halo + 1, tm), :] * cw_ref[2:3, cols]
                        + cb_ref[:, cols])
        return jnp.concatenate(outs, axis=1)

    for j in range(D_FF // TF_FFN):
        off = j * TF_FFN
        base = (j % 2) * 2 * slabs
        hg = jnp.dot(nh, wup_ref[:, pl.ds(off, TF_FFN)], preferred_element_type=jnp.float32)
        hu = jnp.dot(nh, wup_ref[:, pl.ds(D_FF + off, TF_FFN)], preferred_element_type=jnp.float32)
        g = conv(hg, off, base)
        up = conv(hu, D_FF + off, base + slabs)
        act_scr[:, pl.ds(off, TF_FFN)] = (g * jax.nn.sigmoid(g) * up).astype(jnp.bfloat16)
    o_ref[...] = x1[halo:halo + tm] + jnp.dot(act_scr[...], wdn_ref[...], preferred_element_type=jnp.float32)


def _mix_ffn(x2, y, b_n, p, seq_len):
    t = x2.shape[0]
    tm = TM_FFN
    halo = BF16_ROWS
    hb = tm // halo
    nhb = t // halo
    prev_i = lambda i: jnp.maximum(i * hb - 1, 0)
    next_i = lambda i: jnp.minimum((i + 1) * hb, nhb - 1)
    row = lambda w: pl.BlockSpec((tm, w), lambda i: (i, 0))
    rowp = lambda w: pl.BlockSpec((halo, w), lambda i: (prev_i(i), 0))
    rown = lambda w: pl.BlockSpec((halo, w), lambda i: (next_i(i), 0))
    slab = lambda r, f: pl.BlockSpec((S5_SLABS, r, LANES), lambda i: (0, f(i), 0))
    once = lambda shape: pl.BlockSpec(shape, lambda i: (0,) * len(shape), pipeline_mode=pl.Buffered(1))
    return pl.pallas_call(
        functools.partial(_mix_ffn_kernel, tiles_per_seq=seq_len // tm, tm=tm),
        grid=(t // tm,),
        in_specs=[row(D_MODEL), rowp(D_MODEL), rown(D_MODEL),
                  slab(tm, lambda i: i), slab(halo, prev_i), slab(halo, next_i),
                  row(ATTN_WIDTH), rowp(ATTN_WIDTH), rown(ATTN_WIDTH),
                  once((S5_WIDTH, S5_WIDTH)), once((1, S5_WIDTH)), once((1, S5_WIDTH)),
                  once((S5_WIDTH, D_MODEL)), once((ATTN_WIDTH, D_MODEL)), once((1, D_MODEL)),
                  once((D_MODEL, 2 * D_FF)), once((3, 2 * D_FF)), once((1, 2 * D_FF)), once((D_FF, D_MODEL))],
        out_specs=row(D_MODEL),
        out_shape=jax.ShapeDtypeStruct((t, D_MODEL), jnp.float32),
        scratch_shapes=[pltpu.VMEM((4 * TF_FFN // LANES, tm + 2 * halo, LANES), jnp.float32),
                        pltpu.VMEM((tm, D_FF), jnp.bfloat16)],
        compiler_params=_cparams(("parallel",)),
        name="mix_ffn",
    )(x2, x2, x2, y, y, y, b_n, b_n, b_n, p["w_glu"], p["b_glu"], p["ssm_out_g"], p["w_out_a"], p["w_out_b"],
      p["norm_ffn_g"], p["w_up"], p["conv_w"], p["conv_b"], p["w_down"])


def _layer(x, p):
    b, s, d = x.shape
    x2 = x.reshape(b * s, d)
    u, q, k, v = _proj(x2, p["norm_mix_g"], p["w_in"], p["gq"], p["gk"])
    shp = (HEAD_PAIRS, b, s, LANES)
    y = _s5(u.reshape(shp), p["s5_mats"])
    b_n = _attn(q.reshape(shp), k.reshape(shp), v.reshape(shp), p["attn_out_g"])
    out = _mix_ffn(x2, y.reshape(S5_SLABS, b * s, LANES), b_n.reshape(b * s, ATTN_WIDTH), p, s)
    return out.reshape(b, s, d)


def kernel(x_prompt, x_sample, norm_mix_g, w_in, s5_a_re, s5_a_im, s5_log_dt, s5_b_re, s5_b_im, s5_c_re, s5_c_im, s5_d, w_glu, b_glu, q_norm_g, k_norm_g, ssm_out_g, attn_out_g, w_out, norm_ffn_g, w_up, conv_w, conv_b, w_down):
    depth = w_in.shape[0]
    f32, bf16 = jnp.float32, jnp.bfloat16
    y_prompt, y_sample = x_prompt, x_sample
    for i in range(depth):
        p = {
            "norm_mix_g": norm_mix_g[i].astype(f32)[None],
            "w_in": w_in[i].astype(bf16),
            "gq": jnp.tile(q_norm_g[i].astype(f32), N_HEADS)[None],
            "gk": jnp.tile(k_norm_g[i].astype(f32), N_HEADS)[None],
            "s5_mats": _s5_matrices(s5_a_re[i], s5_a_im[i], s5_log_dt[i], s5_b_re[i], s5_b_im[i],
                                    s5_c_re[i], s5_c_im[i], s5_d[i]),
            "w_glu": w_glu[i].astype(bf16),
            "b_glu": b_glu[i].astype(f32)[None],
            "ssm_out_g": ssm_out_g[i].astype(f32)[None],
            "attn_out_g": attn_out_g[i].astype(f32)[None],
            "w_out_a": w_out[i, :S5_WIDTH].astype(bf16),
            "w_out_b": w_out[i, S5_WIDTH:].astype(bf16),
            "norm_ffn_g": norm_ffn_g[i].astype(f32)[None],
            "w_up": w_up[i].astype(bf16),
            "conv_w": conv_w[i].astype(f32),
            "conv_b": conv_b[i].astype(f32)[None],
            "w_down": w_down[i].astype(bf16),
        }
        y_prompt = _layer(y_prompt, p)
        y_sample = _layer(y_sample, p)
    return (y_prompt, y_sample)
```

```python
import functools

import jax
import jax.numpy as jnp
from jax import lax
from jax.experimental import pallas as pl
from jax.experimental.pallas import tpu as pltpu

D_MODEL = 1024
S5_WIDTH = 512
S5_CH = 16
S5_GROUPS = 32
S5_STATE = 64
ATTN_WIDTH = 512
HEAD_DIM = 64
N_HEADS = 8
HEAD_PAIRS = N_HEADS // 2
DILATED_PATTERNS = ((128, 1), (512, 4), (2048, 16))
D_IN = S5_WIDTH + 3 * ATTN_WIDTH
D_FF = 2816
NORM_EPS = 1e-6
NEG_INF = -1e30

LANES = 128
BF16_ROWS = 16
VMEM_LIMIT = 56 * 1024 * 1024

S5_SLABS = S5_WIDTH // LANES
SLAB_GROUPS = LANES // S5_CH
S5_CHUNK = 16
S5_BLOCK = S5_CHUNK * S5_CH
MAX_SCAN_STEPS = 10
S5_LANES = 1024

TM_PROJ = 512
TQ_ATTN = 1024
Q_BLOCK = 128
BLOCK_GROUP = 2
HALF_WIN = 64
TM_FFN = 512
TF_FFN = 256

_NT = (((1,), (1,)), ((), ()))


def _cparams(sem):
    return pltpu.CompilerParams(dimension_semantics=sem, vmem_limit_bytes=VMEM_LIMIT)


def _const_spec(shape):
    nd = len(shape)
    return pl.BlockSpec(shape, lambda *_: (0,) * nd)


def _proj_kernel(x_ref, g_ref, w_ref, gq_ref, gk_ref, u_ref, q_ref, k_ref, v_ref):
    x = x_ref[...]
    ms = jnp.mean(x * x, axis=-1, keepdims=True)
    n = (x * lax.rsqrt(ms + NORM_EPS) * g_ref[...]).astype(jnp.bfloat16)
    proj = jnp.dot(n, w_ref[...], preferred_element_type=jnp.float32)

    first_head = lax.broadcasted_iota(jnp.int32, (1, LANES), 1) < HEAD_DIM

    def head_norm(t, gain):
        sq = t * t
        tots = []
        for s in range(HEAD_PAIRS):
            blk = sq[:, s * LANES:(s + 1) * LANES]
            first = jnp.sum(jnp.where(first_head, blk, 0.0), axis=-1, keepdims=True)
            second = jnp.sum(jnp.where(first_head, 0.0, blk), axis=-1, keepdims=True)
            tots.append(jnp.where(first_head, first, second))
        tot = jnp.concatenate(tots, axis=1)
        return t * lax.rsqrt(tot * (1.0 / HEAD_DIM) + NORM_EPS) * gain

    u = proj[:, :S5_WIDTH]
    q = proj[:, S5_WIDTH:S5_WIDTH + ATTN_WIDTH]
    k = proj[:, S5_WIDTH + ATTN_WIDTH:S5_WIDTH + 2 * ATTN_WIDTH]
    q = head_norm(q, gq_ref[...]) * (HEAD_DIM ** -0.5)
    k = head_norm(k, gk_ref[...])
    v = proj[:, S5_WIDTH + 2 * ATTN_WIDTH:]
    for s in range(S5_SLABS):
        cols = slice(s * LANES, (s + 1) * LANES)
        u_ref[s] = u[:, cols]
        q_ref[s] = q[:, cols]
        k_ref[s] = k[:, cols]
        v_ref[s] = v[:, cols]


def _proj(x2, g, w_in, gq, gk):
    t = x2.shape[0]
    tm = TM_PROJ
    slab = pl.BlockSpec((HEAD_PAIRS, tm, LANES), lambda i: (0, i, 0))
    out = jax.ShapeDtypeStruct((HEAD_PAIRS, t, LANES), jnp.float32)
    return pl.pallas_call(
        _proj_kernel,
        grid=(t // tm,),
        in_specs=[pl.BlockSpec((tm, D_MODEL), lambda i: (i, 0)), _const_spec((1, D_MODEL)),
                  _const_spec((D_MODEL, D_IN)), _const_spec((1, ATTN_WIDTH)), _const_spec((1, ATTN_WIDTH))],
        out_specs=[slab, slab, slab, slab],
        out_shape=[out, out, out, out],
        compiler_params=_cparams(("parallel",)),
        name="proj",
    )(x2, g, w_in, gq, gk)


def _chunk_scan(xr, xi, tab_ref, gi, col0, pos, n_chunks, reverse):
    lanes = xr.shape[1]
    k, step = 1, 0
    while k < n_chunks:
        mr = tab_ref[gi, :, col0 + 2 * step:col0 + 2 * step + 1]
        mi = tab_ref[gi, :, col0 + 2 * step + 1:col0 + 2 * step + 2]
        keep = (pos < n_chunks - k) if reverse else (pos >= k)
        shift = (lanes - k) if reverse else k
        sr = jnp.where(keep, pltpu.roll(xr, shift, 1), 0.0)
        si = jnp.where(keep, pltpu.roll(xi, shift, 1), 0.0)
        xr, xi = xr + (mr * sr - mi * si), xi + (mr * si + mi * sr)
        k, step = 2 * k, step + 1
    return xr, xi


def _s5_kernel(u_ref, toep_ref, p_ref, q_ref, tab_ref, y_ref, xt_scr, yt_scr, *, n_chunks, n_seq):
    n = n_chunks
    lanes = n_seq * n
    for b in range(n_seq):
        for t in range(S5_CHUNK):
            rows = pl.ds(t, n, stride=S5_CHUNK)
            xt_scr[t, :, b * n:(b + 1) * n] = u_ref[0, b, rows, :].T.astype(jnp.bfloat16)
    pos = lax.broadcasted_iota(jnp.int32, (1, lanes), 1) % n
    ns = S5_STATE

    def per_group(gi, _):
        ch = pl.ds(pl.multiple_of(gi * S5_CH, S5_CH), S5_CH)
        x = jnp.concatenate([xt_scr[t, ch, :] for t in range(S5_CHUNK)], axis=0)
        y = jnp.dot(toep_ref[gi], x, preferred_element_type=jnp.float32)
        st = jnp.dot(p_ref[gi], x, preferred_element_type=jnp.float32)
        fr, fi = _chunk_scan(st[0:ns], st[ns:2 * ns], tab_ref, gi, 0, pos, n, False)
        br, bi = _chunk_scan(st[2 * ns:3 * ns], st[3 * ns:4 * ns], tab_ref, gi, 2 * MAX_SCAN_STEPS, pos, n, True)
        prev = lambda a: jnp.where(pos >= 1, pltpu.roll(a, 1, 1), 0.0)
        nxt = lambda a: jnp.where(pos < n - 1, pltpu.roll(a, lanes - 1, 1), 0.0)
        h = jnp.concatenate([prev(fr), prev(fi), nxt(br), nxt(bi)], axis=0).astype(jnp.bfloat16)
        y = y + jnp.dot(q_ref[gi], h, preferred_element_type=jnp.float32)
        for t in range(S5_CHUNK):
            yt_scr[t, ch, :] = y[t * S5_CH:(t + 1) * S5_CH, :]
        return 0

    lax.fori_loop(0, SLAB_GROUPS, per_group, 0)
    for b in range(n_seq):
        for t in range(S5_CHUNK):
            y_ref[0, b, pl.ds(t, n, stride=S5_CHUNK), :] = yt_scr[t, :, b * n:(b + 1) * n].T


def _s5(u, mats):
    _, b, s, _ = u.shape
    n = s // S5_CHUNK
    nb = max(1, min(b, S5_LANES // n))
    toep, pmat, qmat, tab = mats
    io = pl.BlockSpec((1, nb, s, LANES), lambda isl, ib: (isl, ib, 0, 0))
    wspec = pl.BlockSpec((SLAB_GROUPS, S5_BLOCK, S5_BLOCK), lambda isl, ib: (isl, 0, 0))
    return pl.pallas_call(
        functools.partial(_s5_kernel, n_chunks=n, n_seq=nb),
        grid=(S5_SLABS, b // nb),
        in_specs=[io, wspec, wspec, wspec,
                  pl.BlockSpec((SLAB_GROUPS, S5_STATE, LANES), lambda isl, ib: (isl, 0, 0))],
        out_specs=io,
        out_shape=jax.ShapeDtypeStruct(u.shape, jnp.float32),
        scratch_shapes=[pltpu.VMEM((S5_CHUNK, LANES, nb * n), jnp.bfloat16),
                        pltpu.VMEM((S5_CHUNK, LANES, nb * n), jnp.float32)],
        compiler_params=_cparams(("parallel", "parallel")),
        name="s5",
    )(u, toep, pmat, qmat, tab)


def _s5_matrices(a_re, a_im, log_dt, b_re, b_im, c_re, c_im, d_skip):
    f32 = jnp.float32
    hp = lax.Precision.HIGHEST
    n, g_, p_ = S5_CHUNK, S5_GROUPS, S5_STATE
    cmul = lambda a, b: (a[0] * b[0] - a[1] * b[1], a[0] * b[1] + a[1] * b[0])
    kern, p_rows, q_cols, tab_cols = [], [], [], []
    for direction in range(2):
        ar = a_re[direction].astype(f32)
        ai = a_im[direction].astype(f32)
        dt = jnp.exp(log_dt[direction].astype(f32))[:, None]
        mag = jnp.exp(ar * dt)
        lam = (mag * jnp.cos(ai * dt), mag * jnp.sin(ai * dt))
        den = ar * ar + ai * ai
        nr = lam[0] - 1.0
        fr = (nr * ar + lam[1] * ai) / den
        fi = (lam[1] * ar - nr * ai) / den
        br = b_re[direction].astype(f32)
        bi = b_im[direction].astype(f32)
        bbar = (fr[:, :, None] * br - fi[:, :, None] * bi, fr[:, :, None] * bi + fi[:, :, None] * br)
        cr = c_re[direction].astype(f32)
        ci = c_im[direction].astype(f32)
        powers = [(jnp.ones_like(lam[0]), jnp.zeros_like(lam[0]))]
        for _ in range(n):
            powers.append(cmul(powers[-1], lam))
        ks = list(range(n - 1, -1, -1)) if direction == 0 else list(range(n))
        pw = (jnp.stack([powers[k][0] for k in ks], axis=-1)[..., None],
              jnp.stack([powers[k][1] for k in ks], axis=-1)[..., None])
        m = cmul(pw, (bbar[0][:, :, None, :], bbar[1][:, :, None, :]))
        m = (m[0].reshape(g_, p_, S5_BLOCK), m[1].reshape(g_, p_, S5_BLOCK))
        p_rows += [m[0], m[1]]
        kern.append(jnp.einsum('gap,gpx->gax', cr, m[0], precision=hp)
                    - jnp.einsum('gap,gpx->gax', ci, m[1], precision=hp))
        ts = [t + 1 for t in range(n)] if direction == 0 else [n - t for t in range(n)]
        lt = (jnp.stack([powers[k][0] for k in ts], axis=1)[:, :, None, :],
              jnp.stack([powers[k][1] for k in ts], axis=1)[:, :, None, :])
        z = cmul((cr[:, None], ci[:, None]), lt)
        q_cols += [z[0].reshape(g_, S5_BLOCK, p_), -z[1].reshape(g_, S5_BLOCK, p_)]
        step = powers[n]
        for _ in range(MAX_SCAN_STEPS):
            tab_cols += [step[0], step[1]]
            step = cmul(step, step)
    last = (n - 1) * S5_CH
    eye = jnp.eye(S5_CH, dtype=f32)
    centre = kern[0][:, :, last:] + kern[1][:, :, :S5_CH] + d_skip.astype(f32).reshape(g_, S5_CH)[:, :, None] * eye
    by_lag = jnp.concatenate([kern[0][:, :, :last], centre, kern[1][:, :, S5_CH:]], axis=-1)
    toep = jnp.stack([by_lag[:, :, (n - 1 - t) * S5_CH:(n - 1 - t) * S5_CH + S5_BLOCK] for t in range(n)],
                     axis=1).reshape(g_, S5_BLOCK, S5_BLOCK)
    bf16 = jnp.bfloat16
    pmat = jnp.concatenate(p_rows, axis=1)
    qmat = jnp.concatenate(q_cols, axis=2)
    pad = jnp.zeros((g_, p_, LANES - len(tab_cols)), f32)
    tab = jnp.concatenate([jnp.stack(tab_cols, axis=-1), pad], axis=-1)
    return toep.astype(bf16), pmat.astype(bf16), qmat.astype(bf16), tab


def _attn_kernel(q_ref, kin_ref, vin_ref, g_ref, o_ref,
                 o_scr, lse_scr, bias_scr, qd_scr, kd_scr, vd_scr, s_scr, p_scr, kring, vring, *, seq_len, tq):
    j = pl.program_id(1)
    t0 = (j - 1) * tq
    lane = lax.broadcasted_iota(jnp.int32, (1, LANES), 1)
    first_head = lane < HEAD_DIM

    @pl.when(j == 0)
    def _():
        kring[2] = jnp.zeros(kring.shape[1:], kring.dtype)
        vring[2] = jnp.zeros(vring.shape[1:], vring.dtype)

    kring[j % 3] = kin_ref[:, 0]
    vring[j % 3] = vin_ref[:, 0]
    s_prev, s_main, s_next = (j + 1) % 3, (j + 2) % 3, j % 3

    @pl.when((pl.program_id(0) == 0) & (pl.program_id(1) == 0))
    def _():
        for ip, (_, dil) in enumerate(DILATED_PATTERNS):
            qb = min(Q_BLOCK, tq // dil)
            width = qb + 2 * HALF_WIN
            qi = lax.broadcasted_iota(jnp.int32, (qb, width), 0)
            kj = lax.broadcasted_iota(jnp.int32, (qb, width), 1) - HALF_WIN
            rel = jnp.abs(qi - kj)
            dist = (dil * rel).astype(jnp.float32)
            for h in range(N_HEADS):
                slope = 2.0 ** (-8.0 * (h + 1) / N_HEADS)
                bias_scr[ip, h // 2, (h % 2) * qb:(h % 2 + 1) * qb, 0:width] = jnp.where(
                    rel <= HALF_WIN, -slope * dist, NEG_INF)

    def blocks(descs):
        chains = [(bi, hp) for bi in range(len(descs)) for hp in range(HEAD_PAIRS)]

        def stage(c, qb):
            per_slot = Q_BLOCK // qb
            return c // per_slot, (c % per_slot) * 2 * qb

        for c, (bi, hp) in enumerate(chains):
            ip, dil, qb, q_of, k_of, _, _, first_key, check_ends = descs[bi]
            width = qb + 2 * HALF_WIN
            q2 = q_of(hp)
            k2 = k_of(hp)
            qq = jnp.concatenate([jnp.where(first_head, q2, 0.0), jnp.where(first_head, 0.0, q2)],
                                 axis=0).astype(jnp.bfloat16)
            s = lax.dot_general(qq, k2, _NT, preferred_element_type=jnp.float32)
            s = s + bias_scr[ip, hp, 0:2 * qb, 0:width]
            if check_ends:
                kpos = lax.broadcasted_iota(jnp.int32, (1, width), 1) + first_key
                s = jnp.where((kpos >= 0) & (kpos < seq_len // dil), s, NEG_INF)
            slot, r0 = stage(c, qb)
            s_scr[slot, r0:r0 + 2 * qb, 0:width] = s
        stats = []
        for c, (bi, hp) in enumerate(chains):
            qb = descs[bi][2]
            width = qb + 2 * HALF_WIN
            slot, r0 = stage(c, qb)
            s = s_scr[slot, r0:r0 + 2 * qb, 0:width]
            m = jnp.max(s, axis=-1, keepdims=True)
            p = jnp.exp(s - m)
            l = jnp.sum(p, axis=-1, keepdims=True)
            p_scr[slot, r0:r0 + 2 * qb, 0:width] = p.astype(jnp.bfloat16)
            stats.append((1.0 / l, m + jnp.log(l)))
        for c, (bi, hp) in enumerate(chains):
            ip, _, qb, _, _, v_of, out_rows, _, _ = descs[bi]
            width = qb + 2 * HALF_WIN
            inv_l, lse = stats[c]
            slot, r0 = stage(c, qb)
            o2 = jnp.dot(p_scr[slot, r0:r0 + 2 * qb, 0:width], v_of(hp), preferred_element_type=jnp.float32)
            o2 = o2 * inv_l
            o_scr[ip, hp, out_rows, :] = jnp.where(first_head, o2[:qb], o2[qb:])
            lse_scr[ip, hp, out_rows, :] = jnp.where(first_head, lse[:qb], lse[qb:])

    def dense_desc(i0, first, last):
        qb = Q_BLOCK
        own = pl.ds(i0, qb)

        def window(ring, hp):
            lo = (ring[s_prev, hp, pl.ds(tq - HALF_WIN, HALF_WIN), :] if first
                  else ring[s_main, hp, pl.ds(i0 - HALF_WIN, HALF_WIN), :])
            hi = (ring[s_next, hp, pl.ds(0, HALF_WIN), :] if last
                  else ring[s_main, hp, pl.ds(i0 + qb, HALF_WIN), :])
            return jnp.concatenate([lo, ring[s_main, hp, own, :], hi], axis=0).astype(jnp.bfloat16)

        return (0, 1, qb, lambda hp: q_ref[hp, 0, own, :],
                lambda hp: window(kring, hp), lambda hp: window(vring, hp),
                own, t0 + i0 - HALF_WIN, first or last)

    nblk = tq // Q_BLOCK

    def dense_mid(g0, _):
        i0 = pl.multiple_of((BLOCK_GROUP * g0 + 1) * Q_BLOCK, Q_BLOCK)
        blocks([dense_desc(i0 + g * Q_BLOCK, False, False) for g in range(BLOCK_GROUP)])
        return 0

    quarter = tq // 4

    def per_class(r4, _):
        for hp in range(HEAD_PAIRS):
            qd_scr[hp] = q_ref[hp, 0, pl.ds(r4, quarter, stride=4), :]
            for part, slot in enumerate((s_prev, s_main, s_next)):
                dst = pl.ds(part * quarter, quarter)
                kd_scr[hp, dst, :] = kring[slot, hp, pl.ds(r4, quarter, stride=4), :]
                vd_scr[hp, dst, :] = vring[slot, hp, pl.ds(r4, quarter, stride=4), :]

        def desc4(i0):
            win = pl.ds(quarter + i0 - HALF_WIN, Q_BLOCK + 2 * HALF_WIN)
            return (1, 4, Q_BLOCK, lambda hp: qd_scr[hp, pl.ds(i0, Q_BLOCK), :],
                    lambda hp: kd_scr[hp, win, :].astype(jnp.bfloat16),
                    lambda hp: vd_scr[hp, win, :].astype(jnp.bfloat16),
                    pl.ds(r4 + 4 * i0, Q_BLOCK, stride=4), t0 // 4 + i0 - HALF_WIN, True)

        for i0 in range(0, quarter, BLOCK_GROUP * Q_BLOCK):
            blocks([desc4(i0 + g * Q_BLOCK) for g in range(BLOCK_GROUP)])

        n16 = tq // 16

        def desc16(a):
            win = pl.ds(a + quarter - 4 * HALF_WIN, n16 + 2 * HALF_WIN, stride=4)
            return (2, 16, n16, lambda hp: qd_scr[hp, pl.ds(a, n16, stride=4), :],
                    lambda hp: kd_scr[hp, win, :].astype(jnp.bfloat16),
                    lambda hp: vd_scr[hp, win, :].astype(jnp.bfloat16),
                    pl.ds(4 * a + r4, n16, stride=16), t0 // 16 - HALF_WIN, True)

        blocks([desc16(a) for a in range(4)])
        return 0

    @pl.when(j >= 1)
    def _():
        blocks([dense_desc(0, True, False), dense_desc((nblk - 1) * Q_BLOCK, False, True)])
        lax.fori_loop(0, (nblk - 2) // BLOCK_GROUP, dense_mid, 0)
        lax.fori_loop(0, 4, per_class, 0)

        outs = []
        for hp in range(HEAD_PAIRS):
            lses = [lse_scr[ip, hp] for ip in range(len(DILATED_PATTERNS))]
            top = functools.reduce(jnp.maximum, lses)
            ws = [jnp.exp(x - top) for x in lses]
            num = sum(w * o_scr[ip, hp] for ip, w in enumerate(ws))
            outs.append(num / sum(ws))
        o = jnp.concatenate(outs, axis=1)
        ms = jnp.mean(o * o, axis=-1, keepdims=True)
        o_ref[0] = (o * lax.rsqrt(ms + NORM_EPS) * g_ref[...]).astype(o_ref.dtype)


def _attn(q, k, v, g_out):
    _, b, s, _ = q.shape
    tq = TQ_ATTN
    nt = s // tq
    npat = len(DILATED_PATTERNS)
    blk = (HEAD_PAIRS, 1, tq, LANES)
    q_spec = pl.BlockSpec(blk, lambda ib, j: (0, ib, jnp.maximum(j - 1, 0), 0))
    kv_spec = pl.BlockSpec(blk, lambda ib, j: (0, ib, jnp.minimum(j, nt - 1), 0))
    return pl.pallas_call(
        functools.partial(_attn_kernel, seq_len=s, tq=tq),
        grid=(b, nt + 1),
        in_specs=[q_spec, kv_spec, kv_spec, _const_spec((1, ATTN_WIDTH))],
        out_specs=pl.BlockSpec((1, tq, ATTN_WIDTH), lambda ib, j: (ib, jnp.maximum(j - 1, 0), 0)),
        out_shape=jax.ShapeDtypeStruct((b, s, ATTN_WIDTH), jnp.bfloat16),
        scratch_shapes=[pltpu.VMEM((npat, HEAD_PAIRS, tq, LANES), jnp.float32),
                        pltpu.VMEM((npat, HEAD_PAIRS, tq, LANES), jnp.float32),
                        pltpu.VMEM((npat, HEAD_PAIRS, 2 * Q_BLOCK, Q_BLOCK + 2 * HALF_WIN), jnp.float32),
                        pltpu.VMEM((HEAD_PAIRS, tq // 4, LANES), jnp.float32),
                        pltpu.VMEM((HEAD_PAIRS, 3 * tq // 4, LANES), jnp.float32),
                        pltpu.VMEM((HEAD_PAIRS, 3 * tq // 4, LANES), jnp.float32),
                        pltpu.VMEM((BLOCK_GROUP * HEAD_PAIRS, 2 * Q_BLOCK, Q_BLOCK + 2 * HALF_WIN), jnp.float32),
                        pltpu.VMEM((BLOCK_GROUP * HEAD_PAIRS, 2 * Q_BLOCK, Q_BLOCK + 2 * HALF_WIN), jnp.bfloat16),
                        pltpu.VMEM((3, HEAD_PAIRS, tq, LANES), jnp.float32),
                        pltpu.VMEM((3, HEAD_PAIRS, tq, LANES), jnp.float32)],
        compiler_params=_cparams(("arbitrary", "arbitrary")),
        name="attn",
    )(q, k, v, g_out)


def _mix_ffn_kernel(xm_ref, xp_ref, xn_ref, ym_ref, yp_ref, yn_ref, bm_ref, bp_ref, bn_ref,
                    wglu_ref, bglu_ref, ga_ref, wa_ref, wb_ref, gf_ref, wup_ref, cw_ref, cb_ref, wdn_ref,
                    o_ref, h_scr, act_scr, *, tiles_per_seq, tm):
    pos = pl.program_id(0) % tiles_per_seq
    halo = BF16_ROWS
    rows = tm + 2 * halo
    x = jnp.concatenate([xp_ref[...], xm_ref[...], xn_ref[...]], axis=0)
    y = jnp.concatenate(
        [jnp.concatenate([yp_ref[s], ym_ref[s], yn_ref[s]], axis=0) for s in range(S5_SLABS)], axis=1)
    b_n = jnp.concatenate([bp_ref[...], bm_ref[...], bn_ref[...]], axis=0)
    z = jax.nn.gelu(y)
    gate = jax.nn.sigmoid(
        jnp.dot(z.astype(jnp.bfloat16), wglu_ref[...], preferred_element_type=jnp.float32) + bglu_ref[...])
    a = z * gate
    ms = jnp.mean(a * a, axis=-1, keepdims=True)
    a_n = (a * lax.rsqrt(ms + NORM_EPS) * ga_ref[...]).astype(jnp.bfloat16)
    x1 = (x + jnp.dot(a_n, wa_ref[...], preferred_element_type=jnp.float32)
          + jnp.dot(b_n, wb_ref[...], preferred_element_type=jnp.float32))
    ms = jnp.mean(x1 * x1, axis=-1, keepdims=True)
    n = x1 * lax.rsqrt(ms + NORM_EPS) * gf_ref[...]
    r = lax.broadcasted_iota(jnp.int32, (rows, 1), 0)
    outside = ((r < halo) & (pos == 0)) | ((r >= halo + tm) & (pos == tiles_per_seq - 1))
    nh = jnp.where(outside, 0.0, n).astype(jnp.bfloat16)
    slabs = TF_FFN // LANES

    def conv(h, off, base):
        outs = []
        for s in range(slabs):
            h_scr[base + s] = h[:, s * LANES:(s + 1) * LANES]
            cols = pl.ds(off + s * LANES, LANES)
            outs.append(h_scr[base + s, pl.ds(halo - 1, tm), :] * cw_ref[0:1, cols]
                        + h_scr[base + s, pl.ds(halo, tm), :] * cw_ref[1:2, cols]
                        + h_scr[base + s, pl.ds(halo + 1, tm), :] * cw_ref[2:3, cols]
                        + cb_ref[:, cols])
        return jnp.concatenate(outs, axis=1)

    for j in range(D_FF // TF_FFN):
        off = j * TF_FFN
        base = (j % 2) * 2 * slabs
        hg = jnp.dot(nh, wup_ref[:, pl.ds(off, TF_FFN)], preferred_element_type=jnp.float32)
        hu = jnp.dot(nh, wup_ref[:, pl.ds(D_FF + off, TF_FFN)], preferred_element_type=jnp.float32)
        g = conv(hg, off, base)
        up = conv(hu, D_FF + off, base + slabs)
        act_scr[:, pl.ds(off, TF_FFN)] = (g * jax.nn.sigmoid(g) * up).astype(jnp.bfloat16)
    o_ref[...] = x1[halo:halo + tm] + jnp.dot(act_scr[...], wdn_ref[...], preferred_element_type=jnp.float32)


def _mix_ffn(x2, y, b_n, p, seq_len):
    t = x2.shape[0]
    tm = TM_FFN
    halo = BF16_ROWS
    hb = tm // halo
    nhb = t // halo
    prev_i = lambda i: jnp.maximum(i * hb - 1, 0)
    next_i = lambda i: jnp.minimum((i + 1) * hb, nhb - 1)
    row = lambda w: pl.BlockSpec((tm, w), lambda i: (i, 0))
    rowp = lambda w: pl.BlockSpec((halo, w), lambda i: (prev_i(i), 0))
    rown = lambda w: pl.BlockSpec((halo, w), lambda i: (next_i(i), 0))
    slab = lambda r, f: pl.BlockSpec((S5_SLABS, r, LANES), lambda i: (0, f(i), 0))
    once = lambda shape: pl.BlockSpec(shape, lambda i: (0,) * len(shape), pipeline_mode=pl.Buffered(1))
    return pl.pallas_call(
        functools.partial(_mix_ffn_kernel, tiles_per_seq=seq_len // tm, tm=tm),
        grid=(t // tm,),
        in_specs=[row(D_MODEL), rowp(D_MODEL), rown(D_MODEL),
                  slab(tm, lambda i: i), slab(halo, prev_i), slab(halo, next_i),
                  row(ATTN_WIDTH), rowp(ATTN_WIDTH), rown(ATTN_WIDTH),
                  once((S5_WIDTH, S5_WIDTH)), once((1, S5_WIDTH)), once((1, S5_WIDTH)),
                  once((S5_WIDTH, D_MODEL)), once((ATTN_WIDTH, D_MODEL)), once((1, D_MODEL)),
                  once((D_MODEL, 2 * D_FF)), once((3, 2 * D_FF)), once((1, 2 * D_FF)), once((D_FF, D_MODEL))],
        out_specs=row(D_MODEL),
        out_shape=jax.ShapeDtypeStruct((t, D_MODEL), jnp.float32),
        scratch_shapes=[pltpu.VMEM((4 * TF_FFN // LANES, tm + 2 * halo, LANES), jnp.float32),
                        pltpu.VMEM((tm, D_FF), jnp.bfloat16)],
        compiler_params=_cparams(("parallel",)),
        name="mix_ffn",
    )(x2, x2, x2, y, y, y, b_n, b_n, b_n, p["w_glu"], p["b_glu"], p["ssm_out_g"], p["w_out_a"], p["w_out_b"],
      p["norm_ffn_g"], p["w_up"], p["conv_w"], p["conv_b"], p["w_down"])


def _layer(x, p):
    b, s, d = x.shape
    x2 = x.reshape(b * s, d)
    u, q, k, v = _proj(x2, p["norm_mix_g"], p["w_in"], p["gq"], p["gk"])
    shp = (HEAD_PAIRS, b, s, LANES)
    y = _s5(u.reshape(shp), p["s5_mats"])
    b_n = _attn(q.reshape(shp), k.reshape(shp), v.reshape(shp), p["attn_out_g"])
    out = _mix_ffn(x2, y.reshape(S5_SLABS, b * s, LANES), b_n.reshape(b * s, ATTN_WIDTH), p, s)
    return out.reshape(b, s, d)


def kernel(x_prompt, x_sample, norm_mix_g, w_in, s5_a_re, s5_a_im, s5_log_dt, s5_b_re, s5_b_im, s5_c_re, s5_c_im, s5_d, w_glu, b_glu, q_norm_g, k_norm_g, ssm_out_g, attn_out_g, w_out, norm_ffn_g, w_up, conv_w, conv_b, w_down):
    depth = w_in.shape[0]
    f32, bf16 = jnp.float32, jnp.bfloat16
    y_prompt, y_sample = x_prompt, x_sample
    for i in range(depth):
        p = {
            "norm_mix_g": norm_mix_g[i].astype(f32)[None],
            "w_in": w_in[i].astype(bf16),
            "gq": jnp.tile(q_norm_g[i].astype(f32), N_HEADS)[None],
            "gk": jnp.tile(k_norm_g[i].astype(f32), N_HEADS)[None],
            "s5_mats": _s5_matrices(s5_a_re[i], s5_a_im[i], s5_log_dt[i], s5_b_re[i], s5_b_im[i],
                                    s5_c_re[i], s5_c_im[i], s5_d[i]),
            "w_glu": w_glu[i].astype(bf16),
            "b_glu": b_glu[i].astype(f32)[None],
            "ssm_out_g": ssm_out_g[i].astype(f32)[None],
            "attn_out_g": attn_out_g[i].astype(f32)[None],
            "w_out_a": w_out[i, :S5_WIDTH].astype(bf16),
            "w_out_b": w_out[i, S5_WIDTH:].astype(bf16),
            "norm_ffn_g": norm_ffn_g[i].astype(f32)[None],
            "w_up": w_up[i].astype(bf16),
            "conv_w": conv_w[i].astype(f32),
            "conv_b": conv_b[i].astype(f32)[None],
            "w_down": w_down[i].astype(bf16),
        }
        y_prompt = _layer(y_prompt, p)
        y_sample = _layer(y_sample, p)
    return (y_prompt, y_sample)
```
